```python
import jax, jax.numpy as jnp
from jax import lax
import numpy as np

D_MODEL = 1024
BATCH = 2
SEQ = 8192
DEPTH = 1

LRU_WIDTH = D_MODEL
LRU_BLOCKS = 8
LRU_BLOCK_W = LRU_WIDTH // LRU_BLOCKS
LRU_CONV = 4
LRU_C = 8.0
HG_EXPAND = 128
HG_HEADS = D_MODEL // HG_EXPAND
HG_KDIM = HG_HEADS * HG_EXPAND
HG_VHEAD = D_MODEL // HG_HEADS
HG_VDIM = HG_HEADS * HG_VHEAD
HG_CHUNK = 32
D_FF = 2816
FFN_CONV = 3
EPS = 1e-6
IN_SPLITS = (LRU_WIDTH, LRU_WIDTH, HG_KDIM, HG_KDIM, HG_VDIM, HG_VDIM, D_MODEL, D_MODEL)
D_IN = LRU_WIDTH * 2 + HG_KDIM * 2 + HG_VDIM * 2 + D_MODEL * 2

kernel_name = "hybrid_rglru_hgrn2_convffn"


def rmsnorm(x, g):
    xf = x.astype(jnp.float32)
    y = xf * lax.rsqrt(jnp.mean(xf * xf, axis=-1, keepdims=True) + EPS)
    return (y * g.astype(jnp.float32)).astype(x.dtype)


def causal_dwconv(x, w, b):
    k = w.shape[0]
    y = lax.conv_general_dilated(
        x, w[:, None, :].astype(x.dtype), window_strides=(1,), padding=[(k - 1, 0)],
        dimension_numbers=("NWC", "WIO", "NWC"), feature_group_count=x.shape[-1])
    return y + b.astype(x.dtype)


def rg_lru(x, w_a, b_a, w_x, b_x, lam):
    bsz, t, _ = x.shape
    xb = x.astype(jnp.float32).reshape(bsz, t, LRU_BLOCKS, LRU_BLOCK_W)
    r = jax.nn.sigmoid(jnp.einsum("btnd,nde->btne", xb, w_a.astype(jnp.float32))
                       + b_a.astype(jnp.float32).reshape(LRU_BLOCKS, LRU_BLOCK_W))
    i = jax.nn.sigmoid(jnp.einsum("btnd,nde->btne", xb, w_x.astype(jnp.float32))
                       + b_x.astype(jnp.float32).reshape(LRU_BLOCKS, LRU_BLOCK_W))
    log_a = -LRU_C * r * jax.nn.softplus(-lam.astype(jnp.float32)).reshape(LRU_BLOCKS, LRU_BLOCK_W)
    a = jnp.exp(log_a).reshape(bsz, t, LRU_WIDTH)
    u = (jnp.sqrt(-jnp.expm1(2.0 * log_a)) * i * xb).reshape(bsz, t, LRU_WIDTH)

    def combine(left, right):
        a_l, h_l = left
        a_r, h_r = right
        return a_l * a_r, a_r * h_l + h_r

    _, h = lax.associative_scan(combine, (a, u), axis=1)
    return h


def hgrn2(q, fz, v, lb):
    bsz, t, _ = q.shape
    n = t // HG_CHUNK
    q = q.astype(jnp.float32)
    fz = fz.astype(jnp.float32)
    v = v.astype(jnp.float32)
    f = lb + (1.0 - lb) * jax.nn.sigmoid(fz)
    log_f = jnp.log(f)
    k = (1.0 - lb) * jax.nn.sigmoid(-fz)

    def chunks(z, d):
        return z.reshape(bsz, n, HG_CHUNK, HG_HEADS, d).transpose(0, 1, 3, 2, 4)

    qc, kc, gc = chunks(q, HG_EXPAND), chunks(k, HG_EXPAND), chunks(log_f, HG_EXPAND)
    vc = chunks(v, HG_VHEAD)
    g_cum = jnp.cumsum(gc, axis=3)
    g_last = g_cum[:, :, :, -1:, :]
    q_t = qc * jnp.exp(g_cum)
    k_t = kc * jnp.exp(-g_cum)
    k_dec = kc * jnp.exp(g_last - g_cum)
    dec = jnp.exp(g_last[:, :, :, 0, :])

    causal = jnp.tril(jnp.ones((HG_CHUNK, HG_CHUNK), dtype=bool))
    att = jnp.where(causal, jnp.einsum("bnhcd,bnhsd->bnhcs", q_t, k_t), 0.0)
    o_intra = jnp.einsum("bnhcs,bnhse->bnhce", att, vc)

    def step(s, xs):
        q_c, kd_c, v_c, d_c = xs
        o = jnp.einsum("bhcd,bhde->bhce", q_c, s)
        s = s * d_c[..., None] + jnp.einsum("bhcd,bhce->bhde", kd_c, v_c)
        return s, o

    s0 = jnp.zeros((bsz, HG_HEADS, HG_EXPAND, HG_VHEAD), jnp.float32)
    _, o_inter = lax.scan(step, s0, (jnp.moveaxis(q_t, 1, 0), jnp.moveaxis(k_dec, 1, 0),
                                     jnp.moveaxis(vc, 1, 0), jnp.moveaxis(dec, 1, 0)))
    o = o_intra + jnp.moveaxis(o_inter, 0, 1)
    return o.transpose(0, 1, 3, 2, 4).reshape(bsz, t, HG_HEADS, HG_VHEAD)


def setup_inputs(seed: int = 0) -> dict:
    key = jax.random.key(seed)
    ks = jax.random.split(key, 24)
    f32 = jnp.float32

    def nrm(k, shape, fan_in):
        return jax.random.normal(k, shape, f32) * (fan_in ** -0.5)

    def gain(k, shape):
        return 1.0 + 0.02 * jax.random.normal(k, shape, f32)

    def bias(k, shape):
        return 0.02 * jax.random.normal(k, shape, f32)

    u = jax.random.uniform(ks[9], (DEPTH, LRU_WIDTH), f32, 0.9, 0.999)
    s = u ** (1.0 / LRU_C)
    lru_lambda = jnp.log(s) - jnp.log1p(-s)
    return {
        "x": jax.random.normal(ks[0], (BATCH, SEQ, D_MODEL), f32),
        "norm_pre_mix": gain(ks[1], (DEPTH, D_MODEL)),
        "w_in": nrm(ks[2], (DEPTH, D_MODEL, D_IN), D_MODEL),
        "conv_a_w": nrm(ks[3], (DEPTH, LRU_CONV, LRU_WIDTH), LRU_CONV),
        "conv_a_b": bias(ks[4], (DEPTH, LRU_WIDTH)),
        "lru_wa": nrm(ks[5], (DEPTH, LRU_BLOCKS, LRU_BLOCK_W, LRU_BLOCK_W), LRU_BLOCK_W),
        "lru_ba": bias(ks[6], (DEPTH, LRU_WIDTH)),
        "lru_wx": nrm(ks[7], (DEPTH, LRU_BLOCKS, LRU_BLOCK_W, LRU_BLOCK_W), LRU_BLOCK_W),
        "lru_bx": bias(ks[8], (DEPTH, LRU_WIDTH)),
        "lru_lambda": lru_lambda,
        "hg_lb_logits": 0.1 * jax.random.normal(ks[10], (DEPTH + 1, HG_KDIM), f32),
        "hg_norm_g": gain(ks[11], (DEPTH, HG_VDIM)),
        "w_branch_a": nrm(ks[12], (DEPTH, LRU_WIDTH, D_MODEL), LRU_WIDTH),
        "w_branch_b": nrm(ks[13], (DEPTH, HG_VDIM, D_MODEL), HG_VDIM),
        "w_out": nrm(ks[14], (DEPTH, D_MODEL, D_MODEL), D_MODEL),
        "norm_post_mix": gain(ks[15], (DEPTH, D_MODEL)),
        "norm_pre_ffn": gain(ks[16], (DEPTH, D_MODEL)),
        "w_up": nrm(ks[17], (DEPTH, D_MODEL, 2 * D_FF), D_MODEL),
        "conv_f_w": nrm(ks[18], (DEPTH, FFN_CONV, 2 * D_FF), FFN_CONV),
        "conv_f_b": bias(ks[19], (DEPTH, 2 * D_FF)),
        "w_down": nrm(ks[20], (DEPTH, D_FF, D_MODEL), D_FF),
        "norm_post_ffn": gain(ks[21], (DEPTH, D_MODEL)),
    }


def reference(x, norm_pre_mix, w_in, conv_a_w, conv_a_b, lru_wa, lru_ba, lru_wx, lru_bx,
              lru_lambda, hg_lb_logits, hg_norm_g, w_branch_a, w_branch_b, w_out,
              norm_post_mix, norm_pre_ffn, w_up, conv_f_w, conv_f_b, w_down, norm_post_ffn):
    bsz, t, _ = x.shape
    lb_all = jnp.cumsum(jax.nn.softmax(hg_lb_logits.astype(jnp.float32), axis=0), axis=0)
    split_idx = [int(v) for v in np.cumsum(IN_SPLITS)[:-1]]
    for l in range(DEPTH):
        h = rmsnorm(x, norm_pre_mix[l])
        p = h @ w_in[l].astype(h.dtype)
        xa, ga, q, fz, vi, og, gate_a, gate_b = jnp.split(p, split_idx, axis=-1)

        xa = causal_dwconv(xa, conv_a_w[l], conv_a_b[l])
        ya = rg_lru(xa, lru_wa[l], lru_ba[l], lru_wx[l], lru_bx[l], lru_lambda[l])
        ya = (ya * jax.nn.gelu(ga.astype(jnp.float32), approximate=True)).astype(x.dtype)

        ob = hgrn2(q, fz, vi, lb_all[l])
        ob = ob * lax.rsqrt(jnp.mean(ob * ob, axis=-1, keepdims=True) + EPS)
        ob = ob * hg_norm_g[l].astype(jnp.float32).reshape(HG_HEADS, HG_VHEAD)
        yb = (ob.reshape(bsz, t, HG_VDIM) * jax.nn.silu(og.astype(jnp.float32))).astype(x.dtype)

        za = ya @ w_branch_a[l].astype(ya.dtype)
        zb = yb @ w_branch_b[l].astype(yb.dtype)
        mix = jax.nn.sigmoid(gate_a) * za + jax.nn.sigmoid(gate_b) * zb
        x = x + rmsnorm(mix @ w_out[l].astype(mix.dtype), norm_post_mix[l])

        h2 = rmsnorm(x, norm_pre_ffn[l])
        up = causal_dwconv(h2 @ w_up[l].astype(h2.dtype), conv_f_w[l], conv_f_b[l])
        u_gate, u_val = jnp.split(up, 2, axis=-1)
        y = jax.nn.gelu(u_gate, approximate=True) * u_val
        x = x + rmsnorm(y @ w_down[l].astype(y.dtype), norm_post_ffn[l])
    return x
```

```python
import functools

import jax
import jax.numpy as jnp
from jax import lax
from jax.experimental import pallas as pl
from jax.experimental.pallas import tpu as pltpu

D_MODEL = 1024
LRU_BLOCKS = 8
LRU_BLOCK_W = 128
LRU_CONV = 4
LRU_C = 8.0
HG_HEADS = 8
HG_DK = 128
HG_CHUNK = 32
D_FF = 2816
FFN_CONV = 3
EPS = 1e-6

SUBLANES = 8
LANES = 128
TIME_TILE = 512
COL_CHUNK = 256
N_IN_CHUNKS = 8 * D_MODEL // COL_CHUNK
N_D_CHUNKS = D_MODEL // COL_CHUNK
N_FF_CHUNKS = D_FF // COL_CHUNK
VMEM_LIMIT_BYTES = 56 * 1024 * 1024

BF16 = jnp.bfloat16
F32 = jnp.float32


def _rmsnorm(x, g):
    ms = jnp.mean(x * x, axis=-1, keepdims=True)
    return x * lax.rsqrt(ms + EPS) * g


def _dot(a, b):
    return jnp.dot(a, b, preferred_element_type=F32)


def _dot_nt(a, b):
    return lax.dot_general(a, b, (((1,), (1,)), ((), ())), preferred_element_type=F32)


def _dot_tn(a, b):
    return lax.dot_general(a, b, (((0,), (0,)), ((), ())), preferred_element_type=F32)


def _causal_conv(u, wbuf, halo, w, b, taps):
    t = u.shape[0]
    wbuf[0:SUBLANES, :] = halo[...]
    wbuf[SUBLANES:SUBLANES + t, :] = u
    halo[...] = u[t - SUBLANES:t, :]
    y = u * w[taps - 1:taps, :] + b
    for j in range(taps - 1):
        off = SUBLANES - (taps - 1) + j
        y = y + wbuf[off:off + t, :] * w[j:j + 1, :]
    return y


def _group_prefix_affine(a, u):
    r, c = a.shape
    a3 = a.reshape(r // SUBLANES, SUBLANES, c)
    u3 = u.reshape(r // SUBLANES, SUBLANES, c)
    row = lax.broadcasted_iota(jnp.int32, a3.shape, 1)
    for d in (1, 2, 4):
        m = row >= d
        ar = pltpu.roll(a3, d, 1)
        ur = pltpu.roll(u3, d, 1)
        u3 = jnp.where(m, a3 * ur + u3, u3)
        a3 = jnp.where(m, a3 * ar, a3)
    return a3, u3


def _group_prefix_sum(g):
    r, c = g.shape
    g3 = g.reshape(r // SUBLANES, SUBLANES, c)
    row = lax.broadcasted_iota(jnp.int32, g3.shape, 1)
    for d in (1, 2, 4):
        g3 = jnp.where(row >= d, g3 + pltpu.roll(g3, d, 1), g3)
    return g3.reshape(r, c)


def _lru_scan(a, u, h0, sa, su, hin):
    t = a.shape[0]
    ng = t // SUBLANES
    a1, u1 = _group_prefix_affine(a, u)
    sa[...] = a1.reshape(t, LANES)
    su[...] = u1.reshape(t, LANES)
    ag = sa[pl.ds(SUBLANES - 1, ng, stride=SUBLANES), :]
    ug = su[pl.ds(SUBLANES - 1, ng, stride=SUBLANES), :]
    a2, u2 = _group_prefix_affine(ag, ug)
    row = lax.broadcasted_iota(jnp.int32, a2.shape, 1)
    a2e = jnp.where(row == 0, 1.0, pltpu.roll(a2, 1, 1))
    u2e = jnp.where(row == 0, 0.0, pltpu.roll(u2, 1, 1))
    hc = h0
    for g in range(ng // SUBLANES):
        hin[g * SUBLANES:(g + 1) * SUBLANES, :] = a2e[g] * hc + u2e[g]
        hc = (a2[g] * hc + u2[g])[SUBLANES - 1:SUBLANES, :]

    def apply(k, carry):
        r = pl.multiple_of(k * SUBLANES, SUBLANES)
        hrow = hin[pl.ds(k, 1), :]
        sa[pl.ds(r, SUBLANES), :] = sa[pl.ds(r, SUBLANES), :] * hrow + su[pl.ds(r, SUBLANES), :]
        return carry

    lax.fori_loop(0, ng, apply, 0, unroll=8)
    return hc


def _mixer_kernel(x_ref, g_pre_ref, w_in_ref, cw_ref, cb_ref, wa_ref, ba_ref, wx_ref, bx_ref,
                  lam_ref, lbl_ref, hgg_ref, w_a_ref, w_b_ref, w_out_ref, g_post_ref,
                  o_ref,
                  hbuf, za, zb, wbuf, halo_a, hstate, sstate, sa, su, hin,
                  qbuf, kbuf, gbuf, vbuf, obuf):
    tt = x_ref.shape[1]
    t_idx = pl.program_id(1)

    @pl.when(t_idx == 0)
    def _():
        halo_a[...] = jnp.zeros_like(halo_a)
        hstate[...] = jnp.zeros_like(hstate)
        sstate[...] = jnp.zeros_like(sstate)

    x = x_ref[0]
    hbuf[...] = _rmsnorm(x, g_pre_ref[...]).astype(BF16)

    def lru_chunk(c, carry):
        h = hbuf[...]
        xa = _dot(h, w_in_ref[c])
        ga = _dot(h, w_in_ref[N_D_CHUNKS + c])
        xc = _causal_conv(xa, wbuf, halo_a.at[c], cw_ref[c], cb_ref[c], LRU_CONV)
        ys = []
        for s in range(COL_CHUNK // LRU_BLOCK_W):
            n = c * (COL_CHUNK // LRU_BLOCK_W) + s
            xs = xc[:, s * LRU_BLOCK_W:(s + 1) * LRU_BLOCK_W]
            xs16 = xs.astype(BF16)
            r = jax.nn.sigmoid(_dot(xs16, wa_ref[n]) + ba_ref[n])
            i = jax.nn.sigmoid(_dot(xs16, wx_ref[n]) + bx_ref[n])
            nl = -lam_ref[n]
            softplus = jnp.maximum(nl, 0.0) + jnp.log1p(jnp.exp(-jnp.abs(nl)))
            log_a = (-LRU_C) * r * softplus
            a = jnp.exp(log_a)
            u = jnp.sqrt(-jnp.tanh(log_a) * (a * a + 1.0)) * i * xs
            h_last = _lru_scan(a, u, hstate[n], sa, su, hin)
            hstate[n] = h_last
            gs = ga[:, s * LRU_BLOCK_W:(s + 1) * LRU_BLOCK_W]
            ys.append((sa[...] * jax.nn.gelu(gs, approximate=True)).astype(BF16))
        ya = jnp.concatenate(ys, axis=1)
        contrib = _dot(ya, w_a_ref[c])

        @pl.when(c == 0)
        def _():
            za[...] = contrib

        @pl.when(c > 0)
        def _():
            za[...] += contrib

        return carry

    lax.fori_loop(0, N_D_CHUNKS, lru_chunk, 0)

    def hg_prep(c, carry):
        h = hbuf[...]
        q = _dot(h, w_in_ref[2 * N_D_CHUNKS + c])
        fz = _dot(h, w_in_ref[3 * N_D_CHUNKS + c])
        v = _dot(h, w_in_ref[4 * N_D_CHUNKS + c])
        l0 = lbl_ref[0, c]
        l1 = lbl_ref[1, c]
        m = jnp.maximum(l0, l1)
        e0 = jnp.exp(l0 - m)
        lb = e0 / (e0 + jnp.exp(l1 - m))
        sig = jax.nn.sigmoid(fz)
        f = lb + (1.0 - lb) * sig
        k = (1.0 - lb) * (1.0 - sig)
        qbuf[c] = q
        kbuf[c] = k
        gbuf[c] = _group_prefix_sum(jnp.log(f))
        vbuf[c] = v.astype(BF16)
        return carry

    lax.fori_loop(0, N_D_CHUNKS, hg_prep, 0)

    n_chunks = tt // HG_CHUNK
    causal = (lax.broadcasted_iota(jnp.int32, (HG_CHUNK, HG_CHUNK), 0)
              >= lax.broadcasted_iota(jnp.int32, (HG_CHUNK, HG_CHUNK), 1))

    def hg_chunk(n, carry):
        r0 = pl.multiple_of(n * HG_CHUNK, HG_CHUNK)
        for c in range(N_D_CHUNKS):
            q = qbuf[c, pl.ds(r0, HG_CHUNK), :]
            k = kbuf[c, pl.ds(r0, HG_CHUNK), :]
            p = gbuf[c, pl.ds(r0, HG_CHUNK), :]
            v = vbuf[c, pl.ds(r0, HG_CHUNK), :]
            parts = [p[0:SUBLANES]]
            for j in range(1, HG_CHUNK // SUBLANES):
                parts.append(p[j * SUBLANES:(j + 1) * SUBLANES] + parts[-1][SUBLANES - 1:SUBLANES])
            gc = jnp.concatenate(parts, axis=0)
            glast = gc[HG_CHUNK - 1:HG_CHUNK]
            qt = (q * jnp.exp(gc)).astype(BF16)
            kt = (k * jnp.exp(-gc)).astype(BF16)
            kd = (k * jnp.exp(glast - gc)).astype(BF16)
            dec = jnp.exp(glast)
            outs = []
            for s in range(COL_CHUNK // HG_DK):
                hd = c * (COL_CHUNK // HG_DK) + s
                sl = slice(s * HG_DK, (s + 1) * HG_DK)
                att = _dot_nt(qt[:, sl], kt[:, sl])
                att = jnp.where(causal, att, 0.0).astype(BF16)
                st = sstate[hd]
                o = _dot(att, v[:, sl]) + _dot_nt(qt[:, sl], st.astype(BF16))
                sstate[hd] = st * dec[:, sl] + _dot_tn(v[:, sl], kd[:, sl])
                outs.append(o)
            obuf[c, pl.ds(r0, HG_CHUNK), :] = jnp.concatenate(outs, axis=1)
        return carry

    lax.fori_loop(0, n_chunks, hg_chunk, 0)

    def hg_out(c, carry):
        h = hbuf[...]
        og = _dot(h, w_in_ref[5 * N_D_CHUNKS + c])
        ob = obuf[c]
        ys = []
        for s in range(COL_CHUNK // HG_DK):
            sl = slice(s * HG_DK, (s + 1) * HG_DK)
            o = ob[:, sl]
            ms = jnp.mean(o * o, axis=-1, keepdims=True)
            ys.append(o * lax.rsqrt(ms + EPS))
        yb = jnp.concatenate(ys, axis=1) * hgg_ref[c] * (og * jax.nn.sigmoid(og))
        contrib = _dot(yb.astype(BF16), w_b_ref[c])

        @pl.when(c == 0)
        def _():
            zb[...] = contrib

        @pl.when(c > 0)
        def _():
            zb[...] += contrib

        return carry

    lax.fori_loop(0, N_D_CHUNKS, hg_out, 0)

    acc = None
    for c in range(N_D_CHUNKS):
        h = hbuf[...]
        sl = slice(c * COL_CHUNK, (c + 1) * COL_CHUNK)
        gate_a = _dot(h, w_in_ref[6 * N_D_CHUNKS + c])
        gate_b = _dot(h, w_in_ref[7 * N_D_CHUNKS + c])
        mix = jax.nn.sigmoid(gate_a) * za[:, sl] + jax.nn.sigmoid(gate_b) * zb[:, sl]
        part = _dot(mix.astype(BF16), w_out_ref[c])
        acc = part if acc is None else acc + part
    o_ref[0] = x_ref[0] + _rmsnorm(acc, g_post_ref[...])


def _ffn_kernel(x_ref, g_pre_ref, w_up_ref, cw_ref, cb_ref, w_dn_ref, g_post_ref,
                o_ref,
                hbuf, acc, wbuf, halo):
    t_idx = pl.program_id(1)

    @pl.when(t_idx == 0)
    def _():
        halo[...] = jnp.zeros_like(halo)

    hbuf[...] = _rmsnorm(x_ref[0], g_pre_ref[...]).astype(BF16)

    def ff_chunk(j, carry):
        h = hbuf[...]
        ug = _dot(h, w_up_ref[j])
        uv = _dot(h, w_up_ref[N_FF_CHUNKS + j])
        cg = _causal_conv(ug, wbuf, halo.at[j], cw_ref[j], cb_ref[j], FFN_CONV)
        cv = _causal_conv(uv, wbuf, halo.at[N_FF_CHUNKS + j], cw_ref[N_FF_CHUNKS + j],
                          cb_ref[N_FF_CHUNKS + j], FFN_CONV)
        y = (jax.nn.gelu(cg, approximate=True) * cv).astype(BF16)
        contrib = _dot(y, w_dn_ref[j])

        @pl.when(j == 0)
        def _():
            acc[...] = contrib

        @pl.when(j > 0)
        def _():
            acc[...] += contrib

        return carry

    lax.fori_loop(0, N_FF_CHUNKS, ff_chunk, 0)
    o_ref[0] = x_ref[0] + _rmsnorm(acc[...], g_post_ref[...])


def _resident(shape):
    zeros = (0,) * len(shape)
    return pl.BlockSpec(shape, lambda b, t: zeros, pipeline_mode=pl.Buffered(1))


def _col_chunks(w, n_chunks):
    k = w.shape[0]
    return w.reshape(k, n_chunks, COL_CHUNK).transpose(1, 0, 2)


def _mixer_call(x, g_pre, w_in_c, cw, cb, wa, ba, wx, bx, lam, lbl, hgg, w_a, w_b, w_o, g_post):
    bsz, t, d = x.shape
    tt = TIME_TILE
    consts = (g_pre, w_in_c, cw, cb, wa, ba, wx, bx, lam, lbl, hgg, w_a, w_b, w_o, g_post)
    tile = pl.BlockSpec((1, tt, d), lambda b, i: (b, i, 0))
    return pl.pallas_call(
        _mixer_kernel,
        grid=(bsz, t // tt),
        in_specs=[tile] + [_resident(c.shape) for c in consts],
        out_specs=tile,
        out_shape=jax.ShapeDtypeStruct(x.shape, x.dtype),
        scratch_shapes=[
            pltpu.VMEM((tt, d), BF16),
            pltpu.VMEM((tt, d), F32),
            pltpu.VMEM((tt, d), F32),
            pltpu.VMEM((tt + SUBLANES, COL_CHUNK), F32),
            pltpu.VMEM((N_D_CHUNKS, SUBLANES, COL_CHUNK), F32),
            pltpu.VMEM((LRU_BLOCKS, 1, LRU_BLOCK_W), F32),
            pltpu.VMEM((HG_HEADS, HG_DK, HG_DK), F32),
            pltpu.VMEM((tt, LANES), F32),
            pltpu.VMEM((tt, LANES), F32),
            pltpu.VMEM((tt // SUBLANES, LANES), F32),
            pltpu.VMEM((N_D_CHUNKS, tt, COL_CHUNK), F32),
            pltpu.VMEM((N_D_CHUNKS, tt, COL_CHUNK), F32),
            pltpu.VMEM((N_D_CHUNKS, tt, COL_CHUNK), F32),
            pltpu.VMEM((N_D_CHUNKS, tt, COL_CHUNK), BF16),
            pltpu.VMEM((N_D_CHUNKS, tt, COL_CHUNK), F32),
        ],
        compiler_params=pltpu.CompilerParams(
            dimension_semantics=("arbitrary", "arbitrary"),
            vmem_limit_bytes=VMEM_LIMIT_BYTES),
        name="mixer",
    )(x, *consts)


def _ffn_call(x, g_pre, w_up_c, cw, cb, w_dn_c, g_post):
    bsz, t, d = x.shape
    tt = TIME_TILE
    consts = (g_pre, w_up_c, cw, cb, w_dn_c, g_post)
    tile = pl.BlockSpec((1, tt, d), lambda b, i: (b, i, 0))
    return pl.pallas_call(
        _ffn_kernel,
        grid=(bsz, t // tt),
        in_specs=[tile] + [_resident(c.shape) for c in consts],
        out_specs=tile,
        out_shape=jax.ShapeDtypeStruct(x.shape, x.dtype),
        scratch_shapes=[
            pltpu.VMEM((tt, d), BF16),
            pltpu.VMEM((tt, d), F32),
            pltpu.VMEM((tt + SUBLANES, COL_CHUNK), F32),
            pltpu.VMEM((2 * N_FF_CHUNKS, SUBLANES, COL_CHUNK), F32),
        ],
        compiler_params=pltpu.CompilerParams(
            dimension_semantics=("arbitrary", "arbitrary"),
            vmem_limit_bytes=VMEM_LIMIT_BYTES),
        name="ffn",
    )(x, *consts)


def kernel(x, norm_pre_mix, w_in, conv_a_w, conv_a_b, lru_wa, lru_ba, lru_wx, lru_bx, lru_lambda,
           hg_lb_logits, hg_norm_g, w_branch_a, w_branch_b, w_out, norm_post_mix, norm_pre_ffn,
           w_up, conv_f_w, conv_f_b, w_down, norm_post_ffn):
    depth = w_in.shape[0]
    d = D_MODEL
    assert depth == 1 and hg_lb_logits.shape[0] == 2, "single-layer configuration only"
    for l in range(depth):
        w_in_c = _col_chunks(w_in[l], N_IN_CHUNKS).astype(BF16)
        cw_a = conv_a_w[l].reshape(LRU_CONV, N_D_CHUNKS, COL_CHUNK).transpose(1, 0, 2)
        cb_a = conv_a_b[l].reshape(N_D_CHUNKS, 1, COL_CHUNK)
        lbl = hg_lb_logits.reshape(depth + 1, N_D_CHUNKS, 1, COL_CHUNK)
        x = _mixer_call(
            x, norm_pre_mix[l].reshape(1, d), w_in_c, cw_a, cb_a,
            lru_wa[l].astype(BF16), lru_ba[l].reshape(LRU_BLOCKS, 1, LRU_BLOCK_W),
            lru_wx[l].astype(BF16), lru_bx[l].reshape(LRU_BLOCKS, 1, LRU_BLOCK_W),
            lru_lambda[l].reshape(LRU_BLOCKS, 1, LRU_BLOCK_W), lbl,
            hg_norm_g[l].reshape(N_D_CHUNKS, 1, COL_CHUNK),
            w_branch_a[l].reshape(N_D_CHUNKS, COL_CHUNK, d).astype(BF16),
            w_branch_b[l].reshape(N_D_CHUNKS, COL_CHUNK, d).astype(BF16),
            w_out[l].reshape(N_D_CHUNKS, COL_CHUNK, d).astype(BF16),
            norm_post_mix[l].reshape(1, d))
        w_up_c = _col_chunks(w_up[l], 2 * N_FF_CHUNKS).astype(BF16)
        cw_f = conv_f_w[l].reshape(FFN_CONV, 2 * N_FF_CHUNKS, COL_CHUNK).transpose(1, 0, 2)
        cb_f = conv_f_b[l].reshape(2 * N_FF_CHUNKS, 1, COL_CHUNK)
        x = _ffn_call(
            x, norm_pre_ffn[l].reshape(1, d), w_up_c, cw_f, cb_f,
            w_down[l].reshape(N_FF_CHUNKS, COL_CHUNK, d).astype(BF16),
            norm_post_ffn[l].reshape(1, d))
    return x
```

```python
import functools

import jax
import jax.numpy as jnp
from jax import lax
from jax.experimental import pallas as pl
from jax.experimental.pallas import tpu as pltpu

D_MODEL = 1024
LRU_BLOCKS = 8
LRU_BLOCK_W = 128
LRU_CONV = 4
LRU_C = 8.0
HG_HEADS = 8
HG_DK = 128
HG_CHUNK = 32
D_FF = 2816
FFN_CONV = 3
EPS = 1e-6

SUBLANES = 8
LANES = 128
TIME_TILE = 512
COL_CHUNK = 256
N_IN_CHUNKS = 8 * D_MODEL // COL_CHUNK
N_D_CHUNKS = D_MODEL // COL_CHUNK
N_FF_CHUNKS = D_FF // COL_CHUNK
VMEM_LIMIT_BYTES = 56 * 1024 * 1024

BF16 = jnp.bfloat16
F32 = jnp.float32


def _rmsnorm(x, g):
    ms = jnp.mean(x * x, axis=-1, keepdims=True)
    return x * lax.rsqrt(ms + EPS) * g


def _dot(a, b):
    return jnp.dot(a, b, preferred_element_type=F32)


def _dot_nt(a, b):
    return lax.dot_general(a, b, (((1,), (1,)), ((), ())), preferred_element_type=F32)


def _dot_tn(a, b):
    return lax.dot_general(a, b, (((0,), (0,)), ((), ())), preferred_element_type=F32)


def _causal_conv(u, wbuf, halo, w, b, taps):
    t = u.shape[0]
    wbuf[0:SUBLANES, :] = halo[...]
    wbuf[SUBLANES:SUBLANES + t, :] = u
    halo[...] = u[t - SUBLANES:t, :]
    y = u * w[taps - 1:taps, :] + b
    for j in range(taps - 1):
        off = SUBLANES - (taps - 1) + j
        y = y + wbuf[off:off + t, :] * w[j:j + 1, :]
    return y


def _group_prefix_affine(a, u):
    r, c = a.shape
    a3 = a.reshape(r // SUBLANES, SUBLANES, c)
    u3 = u.reshape(r // SUBLANES, SUBLANES, c)
    row = lax.broadcasted_iota(jnp.int32, a3.shape, 1)
    for d in (1, 2, 4):
        m = row >= d
        ar = pltpu.roll(a3, d, 1)
        ur = pltpu.roll(u3, d, 1)
        u3 = jnp.where(m, a3 * ur + u3, u3)
        a3 = jnp.where(m, a3 * ar, a3)
    return a3, u3


def _group_prefix_sum(g):
    r, c = g.shape
    g3 = g.reshape(r // SUBLANES, SUBLANES, c)
    row = lax.broadcasted_iota(jnp.int32, g3.shape, 1)
    for d in (1, 2, 4):
        g3 = jnp.where(row >= d, g3 + pltpu.roll(g3, d, 1), g3)
    return g3.reshape(r, c)


def _lru_scan(a, u, h0, sa, su, hin):
    t = a.shape[0]
    ng = t // SUBLANES
    a1, u1 = _group_prefix_affine(a, u)
    sa[...] = a1.reshape(t, LANES)
    su[...] = u1.reshape(t, LANES)
    ag = sa[pl.ds(SUBLANES - 1, ng, stride=SUBLANES), :]
    ug = su[pl.ds(SUBLANES - 1, ng, stride=SUBLANES), :]
    a2, u2 = _group_prefix_affine(ag, ug)
    row = lax.broadcasted_iota(jnp.int32, a2.shape, 1)
    a2e = jnp.where(row == 0, 1.0, pltpu.roll(a2, 1, 1))
    u2e = jnp.where(row == 0, 0.0, pltpu.roll(u2, 1, 1))
    hc = h0
    for g in range(ng // SUBLANES):
        hin[g * SUBLANES:(g + 1) * SUBLANES, :] = a2e[g] * hc + u2e[g]
        hc = (a2[g] * hc + u2[g])[SUBLANES - 1:SUBLANES, :]

    def apply(k, carry):
        r = pl.multiple_of(k * SUBLANES, SUBLANES)
        hrow = hin[pl.ds(k, 1), :]
        sa[pl.ds(r, SUBLANES), :] = sa[pl.ds(r, SUBLANES), :] * hrow + su[pl.ds(r, SUBLANES), :]
        return carry

    lax.fori_loop(0, ng, apply, 0, unroll=8)
    return hc


def _mixer_kernel(x_ref, g_pre_ref, w_in_ref, cw_ref, cb_ref, wa_ref, ba_ref, wx_ref, bx_ref,
                  lam_ref, lbl_ref, hgg_ref, w_a_ref, w_b_ref, w_out_ref, g_post_ref,
                  o_ref,
                  hbuf, za, zb, wbuf, halo_a, hstate, sstate, sa, su, hin,
                  qbuf, kbuf, gbuf, vbuf, obuf):
    tt = x_ref.shape[1]
    t_idx = pl.program_id(1)

    @pl.when(t_idx == 0)
    def _():
        halo_a[...] = jnp.zeros_like(halo_a)
        hstate[...] = jnp.zeros_like(hstate)
        sstate[...] = jnp.zeros_like(sstate)

    x = x_ref[0]
    hbuf[...] = _rmsnorm(x, g_pre_ref[...]).astype(BF16)

    def lru_chunk(c, carry):
        h = hbuf[...]
        xa = _dot(h, w_in_ref[c])
        ga = _dot(h, w_in_ref[N_D_CHUNKS + c])
        xc = _causal_conv(xa, wbuf, halo_a.at[c], cw_ref[c], cb_ref[c], LRU_CONV)
        ys = []
        for s in range(COL_CHUNK // LRU_BLOCK_W):
            n = c * (COL_CHUNK // LRU_BLOCK_W) + s
            xs = xc[:, s * LRU_BLOCK_W:(s + 1) * LRU_BLOCK_W]
            xs16 = xs.astype(BF16)
            r = jax.nn.sigmoid(_dot(xs16, wa_ref[n]) + ba_ref[n])
            i = jax.nn.sigmoid(_dot(xs16, wx_ref[n]) + bx_ref[n])
            nl = -lam_ref[n]
            softplus = jnp.maximum(nl, 0.0) + jnp.log1p(jnp.exp(-jnp.abs(nl)))
            log_a = (-LRU_C) * r * softplus
            a = jnp.exp(log_a)
            u = jnp.sqrt(-jnp.tanh(log_a) * (a * a + 1.0)) * i * xs
            h_last = _lru_scan(a, u, hstate[n], sa, su, hin)
            hstate[n] = h_last
            gs = ga[:, s * LRU_BLOCK_W:(s + 1) * LRU_BLOCK_W]
            ys.append((sa[...] * jax.nn.gelu(gs, approximate=True)).astype(BF16))
        ya = jnp.concatenate(ys, axis=1)
        contrib = _dot(ya, w_a_ref[c])

        @pl.when(c == 0)
        def _():
            za[...] = contrib

        @pl.when(c > 0)
        def _():
            za[...] += contrib

        return carry

    lax.fori_loop(0, N_D_CHUNKS, lru_chunk, 0)

    def hg_prep(c, carry):
        h = hbuf[...]
        q = _dot(h, w_in_ref[2 * N_D_CHUNKS + c])
        fz = _dot(h, w_in_ref[3 * N_D_CHUNKS + c])
        v = _dot(h, w_in_ref[4 * N_D_CHUNKS + c])
        l0 = lbl_ref[0, c]
        l1 = lbl_ref[1, c]
        m = jnp.maximum(l0, l1)
        e0 = jnp.exp(l0 - m)
        lb = e0 / (e0 + jnp.exp(l1 - m))
        sig = jax.nn.sigmoid(fz)
        f = lb + (1.0 - lb) * sig
        k = (1.0 - lb) * (1.0 - sig)
        qbuf[c] = q
        kbuf[c] = k
        gbuf[c] = _group_prefix_sum(jnp.log(f))
        vbuf[c] = v.astype(BF16)
        return carry

    lax.fori_loop(0, N_D_CHUNKS, hg_prep, 0)

    n_chunks = tt // HG_CHUNK
    causal = (lax.broadcasted_iota(jnp.int32, (HG_CHUNK, HG_CHUNK), 0)
              >= lax.broadcasted_iota(jnp.int32, (HG_CHUNK, HG_CHUNK), 1))

    def hg_chunk(n, carry):
        r0 = pl.multiple_of(n * HG_CHUNK, HG_CHUNK)
        for c in range(N_D_CHUNKS):
            q = qbuf[c, pl.ds(r0, HG_CHUNK), :]
            k = kbuf[c, pl.ds(r0, HG_CHUNK), :]
            p = gbuf[c, pl.ds(r0, HG_CHUNK), :]
            v = vbuf[c, pl.ds(r0, HG_CHUNK), :]
            parts = [p[0:SUBLANES]]
            for j in range(1, HG_CHUNK // SUBLANES):
                parts.append(p[j * SUBLANES:(j + 1) * SUBLANES] + parts[-1][SUBLANES - 1:SUBLANES])
            gc = jnp.concatenate(parts, axis=0)
            glast = gc[HG_CHUNK - 1:HG_CHUNK]
            qt = (q * jnp.exp(gc)).astype(BF16)
            kt = (k * jnp.exp(-gc)).astype(BF16)
            kd = (k * jnp.exp(glast - gc)).astype(BF16)
            dec = jnp.exp(glast)
            outs = []
            for s in range(COL_CHUNK // HG_DK):
                hd = c * (COL_CHUNK // HG_DK) + s
                sl = slice(s * HG_DK, (s + 1) * HG_DK)
                att = _dot_nt(qt[:, sl], kt[:, sl])
                att = jnp.where(causal, att, 0.0).astype(BF16)
                st = sstate[hd]
                o = _dot(att, v[:, sl]) + _dot_nt(qt[:, sl], st.astype(BF16))
                sstate[hd] = st * dec[:, sl] + _dot_tn(v[:, sl], kd[:, sl])
                outs.append(o)
            obuf[c, pl.ds(r0, HG_CHUNK), :] = jnp.concatenate(outs, axis=1)
        return carry

    lax.fori_loop(0, n_chunks, hg_chunk, 0)

    def hg_out(c, carry):
        h = hbuf[...]
        og = _dot(h, w_in_ref[5 * N_D_CHUNKS + c])
        ob = obuf[c]
        ys = []
        for s in range(COL_CHUNK // HG_DK):
            sl = slice(s * HG_DK, (s + 1) * HG_DK)
            o = ob[:, sl]
            ms = jnp.mean(o * o, axis=-1, keepdims=True)
            ys.append(o * lax.rsqrt(ms + EPS))
        yb = jnp.concatenate(ys, axis=1) * hgg_ref[c] * (og * jax.nn.sigmoid(og))
        contrib = _dot(yb.astype(BF16), w_b_ref[c])

        @pl.when(c == 0)
        def _():
            zb[...] = contrib

        @pl.when(c > 0)
        def _():
            zb[...] += contrib

        return carry

    lax.fori_loop(0, N_D_CHUNKS, hg_out, 0)

    acc = None
    for c in range(N_D_CHUNKS):
        h = hbuf[...]
        sl = slice(c * COL_CHUNK, (c + 1) * COL_CHUNK)
        gate_a = _dot(h, w_in_ref[6 * N_D_CHUNKS + c])
        gate_b = _dot(h, w_in_ref[7 * N_D_CHUNKS + c])
        mix = jax.nn.sigmoid(gate_a) * za[:, sl] + jax.nn.sigmoid(gate_b) * zb[:, sl]
        part = _dot(mix.astype(BF16), w_out_ref[c])
        acc = part if acc is None else acc + part
    o_ref[0] = x_ref[0] + _rmsnorm(acc, g_post_ref[...])


def _ffn_kernel(x_ref, g_pre_ref, w_up_ref, cw_ref, cb_ref, w_dn_ref, g_post_ref,
                o_ref,
                hbuf, ybuf, wbuf, halo):
    t_idx = pl.program_id(1)

    @pl.when(t_idx == 0)
    def _():
        halo[...] = jnp.zeros_like(halo)

    hbuf[...] = _rmsnorm(x_ref[0], g_pre_ref[...]).astype(BF16)

    for j in range(N_FF_CHUNKS):
        h = hbuf[...]
        jv = N_FF_CHUNKS + j
        ug = _dot(h, w_up_ref[j])
        uv = _dot(h, w_up_ref[jv])
        cg = _causal_conv(ug, wbuf.at[2 * (j % 2)], halo.at[j], cw_ref[j], cb_ref[j], FFN_CONV)
        cv = _causal_conv(uv, wbuf.at[2 * (j % 2) + 1], halo.at[jv], cw_ref[jv], cb_ref[jv],
                          FFN_CONV)
        ybuf[:, j * COL_CHUNK:(j + 1) * COL_CHUNK] = (
            jax.nn.gelu(cg, approximate=True) * cv).astype(BF16)

    o_ref[0] = x_ref[0] + _rmsnorm(_dot(ybuf[...], w_dn_ref[...]), g_post_ref[...])


def _resident(shape):
    zeros = (0,) * len(shape)
    return pl.BlockSpec(shape, lambda b, t: zeros, pipeline_mode=pl.Buffered(1))


def _col_chunks(w, n_chunks):
    k = w.shape[0]
    return w.reshape(k, n_chunks, COL_CHUNK).transpose(1, 0, 2)


def _mixer_call(x, g_pre, w_in_c, cw, cb, wa, ba, wx, bx, lam, lbl, hgg, w_a, w_b, w_o, g_post):
    bsz, t, d = x.shape
    tt = TIME_TILE
    consts = (g_pre, w_in_c, cw, cb, wa, ba, wx, bx, lam, lbl, hgg, w_a, w_b, w_o, g_post)
    tile = pl.BlockSpec((1, tt, d), lambda b, i: (b, i, 0))
    return pl.pallas_call(
        _mixer_kernel,
        grid=(bsz, t // tt),
        in_specs=[tile] + [_resident(c.shape) for c in consts],
        out_specs=tile,
        out_shape=jax.ShapeDtypeStruct(x.shape, x.dtype),
        scratch_shapes=[
            pltpu.VMEM((tt, d), BF16),
            pltpu.VMEM((tt, d), F32),
            pltpu.VMEM((tt, d), F32),
            pltpu.VMEM((tt + SUBLANES, COL_CHUNK), F32),
            pltpu.VMEM((N_D_CHUNKS, SUBLANES, COL_CHUNK), F32),
            pltpu.VMEM((LRU_BLOCKS, 1, LRU_BLOCK_W), F32),
            pltpu.VMEM((HG_HEADS, HG_DK, HG_DK), F32),
            pltpu.VMEM((tt, LANES), F32),
            pltpu.VMEM((tt, LANES), F32),
            pltpu.VMEM((tt // SUBLANES, LANES), F32),
            pltpu.VMEM((N_D_CHUNKS, tt, COL_CHUNK), F32),
            pltpu.VMEM((N_D_CHUNKS, tt, COL_CHUNK), F32),
            pltpu.VMEM((N_D_CHUNKS, tt, COL_CHUNK), F32),
            pltpu.VMEM((N_D_CHUNKS, tt, COL_CHUNK), BF16),
            pltpu.VMEM((N_D_CHUNKS, tt, COL_CHUNK), F32),
        ],
        compiler_params=pltpu.CompilerParams(
            dimension_semantics=("arbitrary", "arbitrary"),
            vmem_limit_bytes=VMEM_LIMIT_BYTES),
        name="mixer",
    )(x, *consts)


def _ffn_call(x, g_pre, w_up_c, cw, cb, w_dn_c, g_post):
    bsz, t, d = x.shape
    tt = TIME_TILE
    consts = (g_pre, w_up_c, cw, cb, w_dn_c, g_post)
    tile = pl.BlockSpec((1, tt, d), lambda b, i: (b, i, 0))
    return pl.pallas_call(
        _ffn_kernel,
        grid=(bsz, t // tt),
        in_specs=[tile] + [_resident(c.shape) for c in consts],
        out_specs=tile,
        out_shape=jax.ShapeDtypeStruct(x.shape, x.dtype),
        scratch_shapes=[
            pltpu.VMEM((tt, d), BF16),
            pltpu.VMEM((tt, D_FF), BF16),
            pltpu.VMEM((4, tt + SUBLANES, COL_CHUNK), F32),
            pltpu.VMEM((2 * N_FF_CHUNKS, SUBLANES, COL_CHUNK), F32),
        ],
        compiler_params=pltpu.CompilerParams(
            dimension_semantics=("arbitrary", "arbitrary"),
            vmem_limit_bytes=VMEM_LIMIT_BYTES),
        name="ffn",
    )(x, *consts)


def kernel(x, norm_pre_mix, w_in, conv_a_w, conv_a_b, lru_wa, lru_ba, lru_wx, lru_bx, lru_lambda,
           hg_lb_logits, hg_norm_g, w_branch_a, w_branch_b, w_out, norm_post_mix, norm_pre_ffn,
           w_up, conv_f_w, conv_f_b, w_down, norm_post_ffn):
    depth = w_in.shape[0]
    d = D_MODEL
    assert depth == 1 and hg_lb_logits.shape[0] == 2, "single-layer configuration only"
    for l in range(depth):
        w_in_c = _col_chunks(w_in[l], N_IN_CHUNKS).astype(BF16)
        cw_a = conv_a_w[l].reshape(LRU_CONV, N_D_CHUNKS, COL_CHUNK).transpose(1, 0, 2)
        cb_a = conv_a_b[l].reshape(N_D_CHUNKS, 1, COL_CHUNK)
        lbl = hg_lb_logits.reshape(depth + 1, N_D_CHUNKS, 1, COL_CHUNK)
        x = _mixer_call(
            x, norm_pre_mix[l].reshape(1, d), w_in_c, cw_a, cb_a,
            lru_wa[l].astype(BF16), lru_ba[l].reshape(LRU_BLOCKS, 1, LRU_BLOCK_W),
            lru_wx[l].astype(BF16), lru_bx[l].reshape(LRU_BLOCKS, 1, LRU_BLOCK_W),
            lru_lambda[l].reshape(LRU_BLOCKS, 1, LRU_BLOCK_W), lbl,
            hg_norm_g[l].reshape(N_D_CHUNKS, 1, COL_CHUNK),
            w_branch_a[l].reshape(N_D_CHUNKS, COL_CHUNK, d).astype(BF16),
            w_branch_b[l].reshape(N_D_CHUNKS, COL_CHUNK, d).astype(BF16),
            w_out[l].reshape(N_D_CHUNKS, COL_CHUNK, d).astype(BF16),
            norm_post_mix[l].reshape(1, d))
        w_up_c = _col_chunks(w_up[l], 2 * N_FF_CHUNKS).astype(BF16)
        cw_f = conv_f_w[l].reshape(FFN_CONV, 2 * N_FF_CHUNKS, COL_CHUNK).transpose(1, 0, 2)
        cb_f = conv_f_b[l].reshape(2 * N_FF_CHUNKS, 1, COL_CHUNK)
        x = _ffn_call(
            x, norm_pre_ffn[l].reshape(1, d), w_up_c, cw_f, cb_f,
            w_down[l].astype(BF16),
            norm_post_ffn[l].reshape(1, d))
    return x
```

```python
import jax
import jax.numpy as jnp
from jax import lax
from jax.experimental import pallas as pl
from jax.experimental.pallas import tpu as pltpu

D_MODEL = 1024
LRU_BLOCKS = 8
LRU_BLOCK_W = 128
LRU_CONV = 4
LRU_C = 8.0
HG_HEADS = 8
HG_DK = 128
HG_CHUNK = 32
HG_BLOCK = 128
D_FF = 2816
FFN_CONV = 3
EPS = 1e-6

SUBLANES = 8
LANES = 128
TIME_TILE = 512
COL_CHUNK = 256
N_IN_CHUNKS = 8 * D_MODEL // COL_CHUNK
N_D_CHUNKS = D_MODEL // COL_CHUNK
N_FF_CHUNKS = D_FF // COL_CHUNK
HALVES = COL_CHUNK // LANES
VMEM_LIMIT_BYTES = 56 * 1024 * 1024

BF16 = jnp.bfloat16
F32 = jnp.float32


def _rmsnorm(x, g):
    ms = jnp.mean(x * x, axis=-1, keepdims=True)
    return x * lax.rsqrt(ms + EPS) * g


def _dot(a, b):
    return jnp.dot(a, b, preferred_element_type=F32)


def _dot_nt(a, b):
    return lax.dot_general(a, b, (((1,), (1,)), ((), ())), preferred_element_type=F32)


def _dot_tn(a, b):
    return lax.dot_general(a, b, (((0,), (0,)), ((), ())), preferred_element_type=F32)


def _causal_conv(u, wbuf, halo, w, b, taps):
    t = u.shape[0]
    wbuf[0:SUBLANES, :] = halo[...]
    wbuf[SUBLANES:SUBLANES + t, :] = u
    halo[...] = u[t - SUBLANES:t, :]
    y = u * w[taps - 1:taps, :] + b
    for j in range(taps - 1):
        off = SUBLANES - (taps - 1) + j
        y = y + wbuf[off:off + t, :] * w[j:j + 1, :]
    return y


def _group_prefix_affine(a, u):
    r, c = a.shape
    a3 = a.reshape(r // SUBLANES, SUBLANES, c)
    u3 = u.reshape(r // SUBLANES, SUBLANES, c)
    row = lax.broadcasted_iota(jnp.int32, a3.shape, 1)
    for d in (1, 2, 4):
        m = row >= d
        ar = pltpu.roll(a3, d, 1)
        ur = pltpu.roll(u3, d, 1)
        u3 = jnp.where(m, a3 * ur + u3, u3)
        a3 = jnp.where(m, a3 * ar, a3)
    return a3, u3


def _group_prefix_sum(g):
    r, c = g.shape
    g3 = g.reshape(r // SUBLANES, SUBLANES, c)
    row = lax.broadcasted_iota(jnp.int32, g3.shape, 1)
    for d in (1, 2, 4):
        g3 = jnp.where(row >= d, g3 + pltpu.roll(g3, d, 1), g3)
    return g3.reshape(r, c)


def _lru_scan(a, u, h0, sa, su, hin):
    t = a.shape[0]
    ng = t // SUBLANES
    a1, u1 = _group_prefix_affine(a, u)
    sa[...] = a1.reshape(t, LANES)
    su[...] = u1.reshape(t, LANES)
    ag = sa[pl.ds(SUBLANES - 1, ng, stride=SUBLANES), :]
    ug = su[pl.ds(SUBLANES - 1, ng, stride=SUBLANES), :]
    a2, u2 = _group_prefix_affine(ag, ug)
    row = lax.broadcasted_iota(jnp.int32, a2.shape, 1)
    a2e = jnp.where(row == 0, 1.0, pltpu.roll(a2, 1, 1))
    u2e = jnp.where(row == 0, 0.0, pltpu.roll(u2, 1, 1))
    hc = h0
    for g in range(ng // SUBLANES):
        hin[g * SUBLANES:(g + 1) * SUBLANES, :] = a2e[g] * hc + u2e[g]
        hc = (a2[g] * hc + u2[g])[SUBLANES - 1:SUBLANES, :]
    pieces = []
    for k in range(ng):
        rows = slice(k * SUBLANES, (k + 1) * SUBLANES)
        pieces.append(sa[rows, :] * hin[k:k + 1, :] + su[rows, :])
    return jnp.concatenate(pieces, axis=0), hc


def _mixer_kernel(x_ref, g_pre_ref, w_in_ref, cw_ref, cb_ref, wa_ref, ba_ref, wx_ref, bx_ref,
                  lam_ref, lbl_ref, hgg_ref, w_a_ref, w_b_ref, w_out_ref, g_post_ref,
                  o_ref,
                  hbuf, yabuf, ybbuf, mixbuf, wbuf, halo_a, hstate, sstate, sa, su, hin,
                  qbuf, kbuf, gbuf, vbuf, obuf):
    tt = x_ref.shape[1]
    t_idx = pl.program_id(1)

    @pl.when(t_idx == 0)
    def _():
        halo_a[...] = jnp.zeros_like(halo_a)
        hstate[...] = jnp.zeros_like(hstate)
        sstate[...] = jnp.zeros_like(sstate)

    hbuf[...] = _rmsnorm(x_ref[0], g_pre_ref[...]).astype(BF16)

    def lru_chunk(c):
        h = hbuf[...]
        xa = _dot(h, w_in_ref[c])
        ga = _dot(h, w_in_ref[N_D_CHUNKS + c])
        xc = _causal_conv(xa, wbuf.at[c % 2], halo_a.at[c], cw_ref[c], cb_ref[c], LRU_CONV)
        for s in range(HALVES):
            n = c * HALVES + s
            sl = slice(s * LANES, (s + 1) * LANES)
            xs = xc[:, sl]
            xs16 = xs.astype(BF16)
            r = jax.nn.sigmoid(_dot(xs16, wa_ref[n]) + ba_ref[n])
            i = jax.nn.sigmoid(_dot(xs16, wx_ref[n]) + bx_ref[n])
            nl = -lam_ref[n]
            softplus = jnp.maximum(nl, 0.0) + jnp.log1p(jnp.exp(-jnp.abs(nl)))
            log_a = (-LRU_C) * r * softplus
            a = jnp.exp(log_a)
            u = jnp.sqrt(-jnp.tanh(log_a) * (a * a + 1.0)) * i * xs
            hs, h_last = _lru_scan(a, u, hstate[n], sa.at[s], su.at[s], hin.at[s])
            hstate[n] = h_last
            yabuf[:, n * LANES:(n + 1) * LANES] = (
                hs * jax.nn.gelu(ga[:, sl], approximate=True)).astype(BF16)

    def hg_prep(c):
        h = hbuf[...]
        q = _dot(h, w_in_ref[2 * N_D_CHUNKS + c])
        fz = _dot(h, w_in_ref[3 * N_D_CHUNKS + c])
        v = _dot(h, w_in_ref[4 * N_D_CHUNKS + c])
        l0 = lbl_ref[0, c]
        l1 = lbl_ref[1, c]
        m = jnp.maximum(l0, l1)
        e0 = jnp.exp(l0 - m)
        lb = e0 / (e0 + jnp.exp(l1 - m))
        sig = jax.nn.sigmoid(fz)
        f = lb + (1.0 - lb) * sig
        k = (1.0 - lb) * (1.0 - sig)
        qbuf[c] = q
        kbuf[c] = k
        gbuf[c] = _group_prefix_sum(jnp.log(f))
        vbuf[c] = v.astype(BF16)

    causal = (lax.broadcasted_iota(jnp.int32, (HG_BLOCK, HG_BLOCK), 0)
              >= lax.broadcasted_iota(jnp.int32, (HG_BLOCK, HG_BLOCK), 1))
    subs = HG_BLOCK // HG_CHUNK

    def hg_block(nb):
        rows = slice(nb * HG_BLOCK, (nb + 1) * HG_BLOCK)
        for c in range(N_D_CHUNKS):
            q = qbuf[c, rows, :]
            k = kbuf[c, rows, :]
            p = gbuf[c, rows, :]
            v = vbuf[c, rows, :]
            parts = [p[0:SUBLANES]]
            for j in range(1, HG_BLOCK // SUBLANES):
                parts.append(p[j * SUBLANES:(j + 1) * SUBLANES] + parts[-1][SUBLANES - 1:SUBLANES])
            g = jnp.concatenate(parts, axis=0)
            g_end = g[HG_BLOCK - 1:HG_BLOCK]
            q0 = (q * jnp.exp(g)).astype(BF16)
            kd = (k * jnp.exp(g_end - g)).astype(BF16)
            dec = jnp.exp(g_end)
            qts, kas = [], []
            for a in range(subs):
                ra = slice(a * HG_CHUNK, (a + 1) * HG_CHUNK)
                hi = (a + 1) * HG_CHUNK
                if a == 0:
                    qts.append((q[ra] * jnp.exp(g[ra])).astype(BF16))
                    ka = k[0:hi] * jnp.exp(-g[0:hi])
                else:
                    gs = g[a * HG_CHUNK - 1:a * HG_CHUNK]
                    qts.append((q[ra] * jnp.exp(g[ra] - gs)).astype(BF16))
                    ka = k[0:hi] * jnp.exp(gs - g[0:hi])
                kas.append(jnp.concatenate([ka, k[hi:]], axis=0).astype(BF16) if hi < HG_BLOCK
                           else ka.astype(BF16))
            outs = []
            for s in range(HALVES):
                hd = c * HALVES + s
                sl = slice(s * HG_DK, (s + 1) * HG_DK)
                att = jnp.concatenate(
                    [_dot_nt(qts[a][:, sl], kas[a][:, sl]) for a in range(subs)], axis=0)
                att = jnp.where(causal, att, 0.0).astype(BF16)
                st = sstate[hd]
                outs.append(_dot(att, v[:, sl]) + _dot_nt(q0[:, sl], st.astype(BF16)))
                sstate[hd] = st * dec[:, sl] + _dot_tn(v[:, sl], kd[:, sl])
            obuf[c, rows, :] = jnp.concatenate(outs, axis=1)

    def hg_out(c):
        og = _dot(hbuf[...], w_in_ref[5 * N_D_CHUNKS + c])
        ob = obuf[c]
        ys = []
        for s in range(HALVES):
            o = ob[:, s * HG_DK:(s + 1) * HG_DK]
            ms = jnp.mean(o * o, axis=-1, keepdims=True)
            ys.append(o * lax.rsqrt(ms + EPS))
        yb = jnp.concatenate(ys, axis=1) * hgg_ref[c] * (og * jax.nn.sigmoid(og))
        ybbuf[:, c * COL_CHUNK:(c + 1) * COL_CHUNK] = yb.astype(BF16)

    def merge(c):
        h = hbuf[...]
        gate_a = _dot(h, w_in_ref[6 * N_D_CHUNKS + c])
        gate_b = _dot(h, w_in_ref[7 * N_D_CHUNKS + c])
        za = _dot(yabuf[...], w_a_ref[c])
        zb = _dot(ybbuf[...], w_b_ref[c])
        mix = jax.nn.sigmoid(gate_a) * za + jax.nn.sigmoid(gate_b) * zb
        mixbuf[:, c * COL_CHUNK:(c + 1) * COL_CHUNK] = mix.astype(BF16)

    n_blocks = tt // HG_BLOCK
    for c in range(N_D_CHUNKS):
        hg_prep(c)
    for c in range(max(N_D_CHUNKS, n_blocks)):
        if c < N_D_CHUNKS:
            lru_chunk(c)
        if c < n_blocks:
            hg_block(c)
    for c in range(N_D_CHUNKS):
        hg_out(c)
    for c in range(N_D_CHUNKS):
        merge(c)
    out = _dot(mixbuf[...], w_out_ref[...])
    o_ref[0] = x_ref[0] + _rmsnorm(out, g_post_ref[...])


def _ffn_kernel(x_ref, g_pre_ref, w_up_ref, cw_ref, cb_ref, w_dn_ref, g_post_ref,
                o_ref,
                hbuf, ybuf, wbuf, halo):
    t_idx = pl.program_id(1)

    @pl.when(t_idx == 0)
    def _():
        halo[...] = jnp.zeros_like(halo)

    hbuf[...] = _rmsnorm(x_ref[0], g_pre_ref[...]).astype(BF16)

    for j in range(N_FF_CHUNKS):
        h = hbuf[...]
        jv = N_FF_CHUNKS + j
        ug = _dot(h, w_up_ref[j])
        uv = _dot(h, w_up_ref[jv])
        cg = _causal_conv(ug, wbuf.at[2 * (j % 2)], halo.at[j], cw_ref[j], cb_ref[j], FFN_CONV)
        cv = _causal_conv(uv, wbuf.at[2 * (j % 2) + 1], halo.at[jv], cw_ref[jv], cb_ref[jv],
                          FFN_CONV)
        ybuf[:, j * COL_CHUNK:(j + 1) * COL_CHUNK] = (
            jax.nn.gelu(cg, approximate=True) * cv).astype(BF16)

    o_ref[0] = x_ref[0] + _rmsnorm(_dot(ybuf[...], w_dn_ref[...]), g_post_ref[...])


def _resident(shape):
    zeros = (0,) * len(shape)
    return pl.BlockSpec(shape, lambda b, t: zeros, pipeline_mode=pl.Buffered(1))


def _col_chunks(w, n_chunks):
    k = w.shape[0]
    return w.reshape(k, n_chunks, COL_CHUNK).transpose(1, 0, 2)


def _mixer_call(x, g_pre, w_in_c, cw, cb, wa, ba, wx, bx, lam, lbl, hgg, w_a, w_b, w_o, g_post):
    bsz, t, d = x.shape
    tt = TIME_TILE
    consts = (g_pre, w_in_c, cw, cb, wa, ba, wx, bx, lam, lbl, hgg, w_a, w_b, w_o, g_post)
    tile = pl.BlockSpec((1, tt, d), lambda b, i: (b, i, 0))
    return pl.pallas_call(
        _mixer_kernel,
        grid=(bsz, t // tt),
        in_specs=[tile] + [_resident(c.shape) for c in consts],
        out_specs=tile,
        out_shape=jax.ShapeDtypeStruct(x.shape, x.dtype),
        scratch_shapes=[
            pltpu.VMEM((tt, d), BF16),
            pltpu.VMEM((tt, d), BF16),
            pltpu.VMEM((tt, d), BF16),
            pltpu.VMEM((tt, d), BF16),
            pltpu.VMEM((2, tt + SUBLANES, COL_CHUNK), F32),
            pltpu.VMEM((N_D_CHUNKS, SUBLANES, COL_CHUNK), F32),
            pltpu.VMEM((LRU_BLOCKS, 1, LRU_BLOCK_W), F32),
            pltpu.VMEM((HG_HEADS, HG_DK, HG_DK), F32),
            pltpu.VMEM((HALVES, tt, LANES), F32),
            pltpu.VMEM((HALVES, tt, LANES), F32),
            pltpu.VMEM((HALVES, tt // SUBLANES, LANES), F32),
            pltpu.VMEM((N_D_CHUNKS, tt, COL_CHUNK), F32),
            pltpu.VMEM((N_D_CHUNKS, tt, COL_CHUNK), F32),
            pltpu.VMEM((N_D_CHUNKS, tt, COL_CHUNK), F32),
            pltpu.VMEM((N_D_CHUNKS, tt, COL_CHUNK), BF16),
            pltpu.VMEM((N_D_CHUNKS, tt, COL_CHUNK), F32),
        ],
        compiler_params=pltpu.CompilerParams(
            dimension_semantics=("arbitrary", "arbitrary"),
            vmem_limit_bytes=VMEM_LIMIT_BYTES),
        name="mixer",
    )(x, *consts)


def _ffn_call(x, g_pre, w_up_c, cw, cb, w_dn, g_post):
    bsz, t, d = x.shape
    tt = TIME_TILE
    consts = (g_pre, w_up_c, cw, cb, w_dn, g_post)
    tile = pl.BlockSpec((1, tt, d), lambda b, i: (b, i, 0))
    return pl.pallas_call(
        _ffn_kernel,
        grid=(bsz, t // tt),
        in_specs=[tile] + [_resident(c.shape) for c in consts],
        out_specs=tile,
        out_shape=jax.ShapeDtypeStruct(x.shape, x.dtype),
        scratch_shapes=[
            pltpu.VMEM((tt, d), BF16),
            pltpu.VMEM((tt, D_FF), BF16),
            pltpu.VMEM((4, tt + SUBLANES, COL_CHUNK), F32),
            pltpu.VMEM((2 * N_FF_CHUNKS, SUBLANES, COL_CHUNK), F32),
        ],
        compiler_params=pltpu.CompilerParams(
            dimension_semantics=("arbitrary", "arbitrary"),
            vmem_limit_bytes=VMEM_LIMIT_BYTES),
        name="ffn",
    )(x, *consts)


def kernel(x, norm_pre_mix, w_in, conv_a_w, conv_a_b, lru_wa, lru_ba, lru_wx, lru_bx, lru_lambda,
           hg_lb_logits, hg_norm_g, w_branch_a, w_branch_b, w_out, norm_post_mix, norm_pre_ffn,
           w_up, conv_f_w, conv_f_b, w_down, norm_post_ffn):
    depth = w_in.shape[0]
    d = D_MODEL
    assert depth == 1 and hg_lb_logits.shape[0] == 2, "single-layer configuration only"
    for l in range(depth):
        w_in_c = _col_chunks(w_in[l], N_IN_CHUNKS).astype(BF16)
        cw_a = conv_a_w[l].reshape(LRU_CONV, N_D_CHUNKS, COL_CHUNK).transpose(1, 0, 2)
        cb_a = conv_a_b[l].reshape(N_D_CHUNKS, 1, COL_CHUNK)
        lbl = hg_lb_logits.reshape(depth + 1, N_D_CHUNKS, 1, COL_CHUNK)
        x = _mixer_call(
            x, norm_pre_mix[l].reshape(1, d), w_in_c, cw_a, cb_a,
            lru_wa[l].astype(BF16), lru_ba[l].reshape(LRU_BLOCKS, 1, LRU_BLOCK_W),
            lru_wx[l].astype(BF16), lru_bx[l].reshape(LRU_BLOCKS, 1, LRU_BLOCK_W),
            lru_lambda[l].reshape(LRU_BLOCKS, 1, LRU_BLOCK_W), lbl,
            hg_norm_g[l].reshape(N_D_CHUNKS, 1, COL_CHUNK),
            _col_chunks(w_branch_a[l], N_D_CHUNKS).astype(BF16),
            _col_chunks(w_branch_b[l], N_D_CHUNKS).astype(BF16),
            w_out[l].astype(BF16),
            norm_post_mix[l].reshape(1, d))
        w_up_c = _col_chunks(w_up[l], 2 * N_FF_CHUNKS).astype(BF16)
        cw_f = conv_f_w[l].reshape(FFN_CONV, 2 * N_FF_CHUNKS, COL_CHUNK).transpose(1, 0, 2)
        cb_f = conv_f_b[l].reshape(2 * N_FF_CHUNKS, 1, COL_CHUNK)
        x = _ffn_call(
            x, norm_pre_ffn[l].reshape(1, d), w_up_c, cw_f, cb_f,
            w_down[l].astype(BF16), norm_post_ffn[l].reshape(1, d))
    return x
```

```python
import jax
import jax.numpy as jnp
from jax import lax
from jax.experimental import pallas as pl
from jax.experimental.pallas import tpu as pltpu

D_MODEL = 1024
LRU_BLOCKS = 8
LRU_BLOCK_W = 128
LRU_CONV = 4
LRU_C = 8.0
HG_HEADS = 8
HG_DK = 128
HG_CHUNK = 32
HG_BLOCK = 128
D_FF = 2816
FFN_CONV = 3
EPS = 1e-6

SUBLANES = 8
LANES = 128
TIME_TILE = 512
COL_CHUNK = 256
N_D_CHUNKS = D_MODEL // COL_CHUNK
N_FF_CHUNKS = D_FF // COL_CHUNK
HALVES = COL_CHUNK // LANES
VMEM_LIMIT_BYTES = 56 * 1024 * 1024

SEC_XA, SEC_GA, SEC_Q, SEC_F, SEC_V, SEC_OG, SEC_GATE_A, SEC_GATE_B = range(8)

BF16 = jnp.bfloat16
F32 = jnp.float32


def _cols(c, width=COL_CHUNK, base=0):
    return slice(base + c * width, base + (c + 1) * width)


def _rmsnorm(x, g):
    ms = jnp.mean(x * x, axis=-1, keepdims=True)
    return x * lax.rsqrt(ms + EPS) * g


def _dot(a, b):
    return jnp.dot(a, b, preferred_element_type=F32)


def _dot_nt(a, b):
    return lax.dot_general(a, b, (((1,), (1,)), ((), ())), preferred_element_type=F32)


def _dot_tn(a, b):
    return lax.dot_general(a, b, (((0,), (0,)), ((), ())), preferred_element_type=F32)


def _causal_conv(u, wbuf, halo, w, b, taps):
    t = u.shape[0]
    wbuf[0:SUBLANES, :] = halo[...]
    wbuf[SUBLANES:SUBLANES + t, :] = u
    halo[...] = u[t - SUBLANES:t, :]
    y = u * w[taps - 1:taps, :] + b
    for j in range(taps - 1):
        off = SUBLANES - (taps - 1) + j
        y = y + wbuf[off:off + t, :] * w[j:j + 1, :]
    return y


def _group_prefix_affine(a, u):
    r, c = a.shape
    a3 = a.reshape(r // SUBLANES, SUBLANES, c)
    u3 = u.reshape(r // SUBLANES, SUBLANES, c)
    row = lax.broadcasted_iota(jnp.int32, a3.shape, 1)
    for d in (1, 2, 4):
        m = row >= d
        ar = pltpu.roll(a3, d, 1)
        ur = pltpu.roll(u3, d, 1)
        u3 = jnp.where(m, a3 * ur + u3, u3)
        a3 = jnp.where(m, a3 * ar, a3)
    return a3, u3


def _group_prefix_sum(g):
    r, c = g.shape
    g3 = g.reshape(r // SUBLANES, SUBLANES, c)
    row = lax.broadcasted_iota(jnp.int32, g3.shape, 1)
    for d in (1, 2, 4):
        g3 = jnp.where(row >= d, g3 + pltpu.roll(g3, d, 1), g3)
    return g3.reshape(r, c)


def _lru_scan(a, u, h0, sa, su, hin):
    t = a.shape[0]
    ng = t // SUBLANES
    a1, u1 = _group_prefix_affine(a, u)
    sa[...] = a1.reshape(t, LANES)
    su[...] = u1.reshape(t, LANES)
    ag = sa[pl.ds(SUBLANES - 1, ng, stride=SUBLANES), :]
    ug = su[pl.ds(SUBLANES - 1, ng, stride=SUBLANES), :]
    a2, u2 = _group_prefix_affine(ag, ug)
    row = lax.broadcasted_iota(jnp.int32, a2.shape, 1)
    a2e = jnp.where(row == 0, 1.0, pltpu.roll(a2, 1, 1))
    u2e = jnp.where(row == 0, 0.0, pltpu.roll(u2, 1, 1))
    hc = h0
    for g in range(ng // SUBLANES):
        hin[g * SUBLANES:(g + 1) * SUBLANES, :] = a2e[g] * hc + u2e[g]
        hc = (a2[g] * hc + u2[g])[SUBLANES - 1:SUBLANES, :]
    pieces = []
    for k in range(ng):
        rows = slice(k * SUBLANES, (k + 1) * SUBLANES)
        pieces.append(sa[rows, :] * hin[k:k + 1, :] + su[rows, :])
    return jnp.concatenate(pieces, axis=0), hc


def _mixer_kernel(x_ref, g_pre_ref, w_in_ref, cw_ref, cb_ref, wa_ref, ba_ref, wx_ref, bx_ref,
                  lam_ref, lbl_ref, hgg_ref, w_a_ref, w_b_ref, w_out_ref, g_post_ref,
                  o_ref,
                  hbuf, yabuf, ybbuf, mixbuf, sga, sgb, xcbuf, gabuf, wbuf, halo_a, hstate, sstate,
                  sa, su, hin,
                  qbuf, kbuf, gbuf, vbuf, obuf):
    tt = x_ref.shape[1]
    t_idx = pl.program_id(1)

    @pl.when(t_idx == 0)
    def _():
        halo_a[...] = jnp.zeros_like(halo_a)
        hstate[...] = jnp.zeros_like(hstate)
        sstate[...] = jnp.zeros_like(sstate)

    hbuf[...] = _rmsnorm(x_ref[0], g_pre_ref[...]).astype(BF16)

    def in_proj(section, c):
        return _dot(hbuf[...], w_in_ref[:, _cols(c, base=section * D_MODEL)])

    def lru_front(c):
        cc = _cols(c)
        xa = in_proj(SEC_XA, c)
        xcbuf[c % 2] = _causal_conv(xa, wbuf.at[c % 2], halo_a.at[:, cc], cw_ref[:, cc],
                                    cb_ref[:, cc], LRU_CONV)
        gabuf[c % 2] = jax.nn.gelu(in_proj(SEC_GA, c), approximate=True)

    def lru_half(c, s):
        n = c * HALVES + s
        sl = _cols(s, LANES)
        nn = _cols(n, LANES)
        xs = xcbuf[c % 2, :, sl]
        xs16 = xs.astype(BF16)
        r = jax.nn.sigmoid(_dot(xs16, wa_ref[n]) + ba_ref[:, nn])
        i = jax.nn.sigmoid(_dot(xs16, wx_ref[n]) + bx_ref[:, nn])
        nl = -lam_ref[:, nn]
        softplus = jnp.maximum(nl, 0.0) + jnp.log1p(jnp.exp(-jnp.abs(nl)))
        log_a = (-LRU_C) * r * softplus
        a = jnp.exp(log_a)
        u = jnp.sqrt(-jnp.tanh(log_a) * (a * a + 1.0)) * i * xs
        hs, h_last = _lru_scan(a, u, hstate[:, nn], sa.at[s], su.at[s], hin.at[s])
        hstate[:, nn] = h_last
        yabuf[:, nn] = (hs * gabuf[c % 2, :, sl]).astype(BF16)

    def hg_prep(c):
        pb = c % 2
        cc = _cols(c)
        q = in_proj(SEC_Q, c)
        fz = in_proj(SEC_F, c)
        v = in_proj(SEC_V, c)
        l0 = lbl_ref[0:1, cc]
        l1 = lbl_ref[1:2, cc]
        m = jnp.maximum(l0, l1)
        e0 = jnp.exp(l0 - m)
        lb = e0 / (e0 + jnp.exp(l1 - m))
        sig = jax.nn.sigmoid(fz)
        f = lb + (1.0 - lb) * sig
        k = (1.0 - lb) * (1.0 - sig)
        qbuf[pb] = q
        kbuf[pb] = k
        gbuf[pb] = _group_prefix_sum(jnp.log(f))
        vbuf[pb] = v.astype(BF16)

    causal = (lax.broadcasted_iota(jnp.int32, (HG_BLOCK, HG_BLOCK), 0)
              >= lax.broadcasted_iota(jnp.int32, (HG_BLOCK, HG_BLOCK), 1))
    subs = HG_BLOCK // HG_CHUNK

    def hg_block(c, nb):
        pb = c % 2
        rows = slice(nb * HG_BLOCK, (nb + 1) * HG_BLOCK)
        q = qbuf[pb, rows, :]
        k = kbuf[pb, rows, :]
        p = gbuf[pb, rows, :]
        v = vbuf[pb, rows, :]
        parts = [p[0:SUBLANES]]
        for j in range(1, HG_BLOCK // SUBLANES):
            parts.append(p[j * SUBLANES:(j + 1) * SUBLANES] + parts[-1][SUBLANES - 1:SUBLANES])
        g = jnp.concatenate(parts, axis=0)
        g_end = g[HG_BLOCK - 1:HG_BLOCK]
        q0 = (q * jnp.exp(g)).astype(BF16)
        kd = (k * jnp.exp(g_end - g)).astype(BF16)
        dec = jnp.exp(g_end)
        qts, kas = [], []
        for a in range(subs):
            ra = slice(a * HG_CHUNK, (a + 1) * HG_CHUNK)
            hi = (a + 1) * HG_CHUNK
            if a == 0:
                qts.append((q[ra] * jnp.exp(g[ra])).astype(BF16))
                ka = k[0:hi] * jnp.exp(-g[0:hi])
            else:
                gs = g[a * HG_CHUNK - 1:a * HG_CHUNK]
                qts.append((q[ra] * jnp.exp(g[ra] - gs)).astype(BF16))
                ka = k[0:hi] * jnp.exp(gs - g[0:hi])
            kas.append(jnp.concatenate([ka, k[hi:]], axis=0).astype(BF16) if hi < HG_BLOCK
                       else ka.astype(BF16))
        outs = []
        for s in range(HALVES):
            hd = c * HALVES + s
            sl = _cols(s, HG_DK)
            att = jnp.concatenate(
                [_dot_nt(qts[a][:, sl], kas[a][:, sl]) for a in range(subs)], axis=0)
            att = jnp.where(causal, att, 0.0).astype(BF16)
            st = sstate[hd]
            outs.append(_dot(att, v[:, sl]) + _dot_nt(q0[:, sl], st.astype(BF16)))
            sstate[hd] = st * dec[:, sl] + _dot_tn(v[:, sl], kd[:, sl])
        obuf[pb, rows, :] = jnp.concatenate(outs, axis=1)

    def hg_out(c):
        og = in_proj(SEC_OG, c)
        ob = obuf[c % 2]
        ys = []
        for s in range(HALVES):
            o = ob[:, _cols(s, HG_DK)]
            ms = jnp.mean(o * o, axis=-1, keepdims=True)
            ys.append(o * lax.rsqrt(ms + EPS))
        yb = jnp.concatenate(ys, axis=1) * hgg_ref[:, _cols(c)] * (og * jax.nn.sigmoid(og))
        ybbuf[:, _cols(c)] = yb.astype(BF16)

    def gate_a(c):
        sga[:, _cols(c)] = jax.nn.sigmoid(in_proj(SEC_GATE_A, c))

    def gate_b(c):
        sgb[:, _cols(c)] = jax.nn.sigmoid(in_proj(SEC_GATE_B, c))

    def merge(c):
        cc = _cols(c)
        za = _dot(yabuf[...], w_a_ref[:, cc])
        zb = _dot(ybbuf[...], w_b_ref[:, cc])
        mixbuf[:, cc] = (sga[:, cc] * za + sgb[:, cc] * zb).astype(BF16)

    assert tt // HG_BLOCK == 4
    for c in range(N_D_CHUNKS):
        hg_prep(c)
        lru_front(c)
        hg_block(c, 0)
        lru_half(c, 0)
        hg_block(c, 1)
        gate_a(c)
        lru_half(c, 1)
        hg_block(c, 2)
        gate_b(c)
        hg_block(c, 3)
        hg_out(c)
    for c in range(N_D_CHUNKS):
        merge(c)
    out = _dot(mixbuf[...], w_out_ref[...])
    o_ref[0] = x_ref[0] + _rmsnorm(out, g_post_ref[...])


def _ffn_kernel(x_ref, g_pre_ref, w_up_ref, cw_ref, cb_ref, w_dn_ref, g_post_ref,
                o_ref,
                hbuf, ybuf, wbuf, halo):
    t_idx = pl.program_id(1)

    @pl.when(t_idx == 0)
    def _():
        halo[...] = jnp.zeros_like(halo)

    hbuf[...] = _rmsnorm(x_ref[0], g_pre_ref[...]).astype(BF16)

    for j in range(N_FF_CHUNKS):
        h = hbuf[...]
        cg_cols = _cols(j)
        cv_cols = _cols(j, base=D_FF)
        ug = _dot(h, w_up_ref[:, cg_cols])
        uv = _dot(h, w_up_ref[:, cv_cols])
        cg = _causal_conv(ug, wbuf.at[2 * (j % 2)], halo.at[:, cg_cols], cw_ref[:, cg_cols],
                          cb_ref[:, cg_cols], FFN_CONV)
        cv = _causal_conv(uv, wbuf.at[2 * (j % 2) + 1], halo.at[:, cv_cols], cw_ref[:, cv_cols],
                          cb_ref[:, cv_cols], FFN_CONV)
        ybuf[:, cg_cols] = (jax.nn.gelu(cg, approximate=True) * cv).astype(BF16)

    o_ref[0] = x_ref[0] + _rmsnorm(_dot(ybuf[...], w_dn_ref[...]), g_post_ref[...])


def _resident(shape):
    zeros = (0,) * len(shape)
    return pl.BlockSpec(shape, lambda b, t: zeros, pipeline_mode=pl.Buffered(1))


def _mixer_call(x, *consts):
    bsz, t, d = x.shape
    tt = TIME_TILE
    tile = pl.BlockSpec((1, tt, d), lambda b, i: (b, i, 0))
    return pl.pallas_call(
        _mixer_kernel,
        grid=(bsz, t // tt),
        in_specs=[tile] + [_resident(c.shape) for c in consts],
        out_specs=tile,
        out_shape=jax.ShapeDtypeStruct(x.shape, x.dtype),
        scratch_shapes=[
            pltpu.VMEM((tt, d), BF16),
            pltpu.VMEM((tt, d), BF16),
            pltpu.VMEM((tt, d), BF16),
            pltpu.VMEM((tt, d), BF16),
            pltpu.VMEM((tt, d), F32),
            pltpu.VMEM((tt, d), F32),
            pltpu.VMEM((2, tt, COL_CHUNK), F32),
            pltpu.VMEM((2, tt, COL_CHUNK), F32),
            pltpu.VMEM((2, tt + SUBLANES, COL_CHUNK), F32),
            pltpu.VMEM((SUBLANES, d), F32),
            pltpu.VMEM((1, d), F32),
            pltpu.VMEM((HG_HEADS, HG_DK, HG_DK), F32),
            pltpu.VMEM((HALVES, tt, LANES), F32),
            pltpu.VMEM((HALVES, tt, LANES), F32),
            pltpu.VMEM((HALVES, tt // SUBLANES, LANES), F32),
            pltpu.VMEM((2, tt, COL_CHUNK), F32),
            pltpu.VMEM((2, tt, COL_CHUNK), F32),
            pltpu.VMEM((2, tt, COL_CHUNK), F32),
            pltpu.VMEM((2, tt, COL_CHUNK), BF16),
            pltpu.VMEM((2, tt, COL_CHUNK), F32),
        ],
        compiler_params=pltpu.CompilerParams(
            dimension_semantics=("arbitrary", "arbitrary"),
            vmem_limit_bytes=VMEM_LIMIT_BYTES),
        name="mixer",
    )(x, *consts)


def _ffn_call(x, *consts):
    bsz, t, d = x.shape
    tt = TIME_TILE
    tile = pl.BlockSpec((1, tt, d), lambda b, i: (b, i, 0))
    return pl.pallas_call(
        _ffn_kernel,
        grid=(bsz, t // tt),
        in_specs=[tile] + [_resident(c.shape) for c in consts],
        out_specs=tile,
        out_shape=jax.ShapeDtypeStruct(x.shape, x.dtype),
        scratch_shapes=[
            pltpu.VMEM((tt, d), BF16),
            pltpu.VMEM((tt, D_FF), BF16),
            pltpu.VMEM((4, tt + SUBLANES, COL_CHUNK), F32),
            pltpu.VMEM((SUBLANES, 2 * D_FF), F32),
        ],
        compiler_params=pltpu.CompilerParams(
            dimension_semantics=("arbitrary", "arbitrary"),
            vmem_limit_bytes=VMEM_LIMIT_BYTES),
        name="ffn",
    )(x, *consts)


def kernel(x, norm_pre_mix, w_in, conv_a_w, conv_a_b, lru_wa, lru_ba, lru_wx, lru_bx, lru_lambda,
           hg_lb_logits, hg_norm_g, w_branch_a, w_branch_b, w_out, norm_post_mix, norm_pre_ffn,
           w_up, conv_f_w, conv_f_b, w_down, norm_post_ffn):
    depth = w_in.shape[0]
    d = D_MODEL
    assert depth == 1 and hg_lb_logits.shape[0] == 2, "single-layer configuration only"
    for l in range(depth):
        x = _mixer_call(
            x, norm_pre_mix[l].reshape(1, d), w_in[l].astype(BF16),
            conv_a_w[l], conv_a_b[l].reshape(1, d),
            lru_wa[l].astype(BF16), lru_ba[l].reshape(1, d),
            lru_wx[l].astype(BF16), lru_bx[l].reshape(1, d),
            lru_lambda[l].reshape(1, d), hg_lb_logits, hg_norm_g[l].reshape(1, d),
            w_branch_a[l].astype(BF16), w_branch_b[l].astype(BF16), w_out[l].astype(BF16),
            norm_post_mix[l].reshape(1, d))
        x = _ffn_call(
            x, norm_pre_ffn[l].reshape(1, d), w_up[l].astype(BF16),
            conv_f_w[l], conv_f_b[l].reshape(1, 2 * D_FF),
            w_down[l].astype(BF16), norm_post_ffn[l].reshape(1, d))
    return x
```

```python
import jax
import jax.numpy as jnp
from jax import lax
from jax.experimental import pallas as pl
from jax.experimental.pallas import tpu as pltpu

D_MODEL = 1024
LRU_BLOCKS = 8
LRU_BLOCK_W = 128
LRU_CONV = 4
LRU_C = 8.0
HG_HEADS = 8
HG_DK = 128
HG_CHUNK = 32
HG_BLOCK = 128
D_FF = 2816
FFN_CONV = 3
EPS = 1e-6

SUBLANES = 8
LANES = 128
TIME_TILE = 512
COL_CHUNK = 256
N_D_CHUNKS = D_MODEL // COL_CHUNK
N_FF_CHUNKS = D_FF // COL_CHUNK
HALVES = COL_CHUNK // LANES
VMEM_LIMIT_BYTES = 56 * 1024 * 1024

SEC_XA, SEC_GA, SEC_Q, SEC_F, SEC_V, SEC_OG, SEC_GATE_A, SEC_GATE_B = range(8)

BF16 = jnp.bfloat16
F32 = jnp.float32


def _cols(c, width=COL_CHUNK, base=0):
    return slice(base + c * width, base + (c + 1) * width)


def _rmsnorm(x, g):
    ms = jnp.mean(x * x, axis=-1, keepdims=True)
    return x * lax.rsqrt(ms + EPS) * g


def _dot(a, b):
    return jnp.dot(a, b, preferred_element_type=F32)


def _dot_nt(a, b):
    return lax.dot_general(a, b, (((1,), (1,)), ((), ())), preferred_element_type=F32)


def _dot_tn(a, b):
    return lax.dot_general(a, b, (((0,), (0,)), ((), ())), preferred_element_type=F32)


def _causal_conv(u, wbuf, halo, w, b, taps):
    t = u.shape[0]
    wbuf[0:SUBLANES, :] = halo[...]
    wbuf[SUBLANES:SUBLANES + t, :] = u
    halo[...] = u[t - SUBLANES:t, :]
    y = u * w[taps - 1:taps, :] + b
    for j in range(taps - 1):
        off = SUBLANES - (taps - 1) + j
        y = y + wbuf[off:off + t, :] * w[j:j + 1, :]
    return y


def _group_prefix_affine(a, u):
    r, c = a.shape
    a3 = a.reshape(r // SUBLANES, SUBLANES, c)
    u3 = u.reshape(r // SUBLANES, SUBLANES, c)
    row = lax.broadcasted_iota(jnp.int32, a3.shape, 1)
    for d in (1, 2, 4):
        m = row >= d
        ar = pltpu.roll(a3, d, 1)
        ur = pltpu.roll(u3, d, 1)
        u3 = jnp.where(m, a3 * ur + u3, u3)
        a3 = jnp.where(m, a3 * ar, a3)
    return a3, u3


def _group_prefix_sum(g):
    r, c = g.shape
    g3 = g.reshape(r // SUBLANES, SUBLANES, c)
    row = lax.broadcasted_iota(jnp.int32, g3.shape, 1)
    for d in (1, 2, 4):
        g3 = jnp.where(row >= d, g3 + pltpu.roll(g3, d, 1), g3)
    return g3.reshape(r, c)


def _lru_scan(a, u, h0, sa, su, hin):
    t = a.shape[0]
    ng = t // SUBLANES
    a1, u1 = _group_prefix_affine(a, u)
    sa[...] = a1.reshape(t, LANES)
    su[...] = u1.reshape(t, LANES)
    ag = sa[pl.ds(SUBLANES - 1, ng, stride=SUBLANES), :]
    ug = su[pl.ds(SUBLANES - 1, ng, stride=SUBLANES), :]
    a2, u2 = _group_prefix_affine(ag, ug)
    row = lax.broadcasted_iota(jnp.int32, a2.shape, 1)
    a2e = jnp.where(row == 0, 1.0, pltpu.roll(a2, 1, 1))
    u2e = jnp.where(row == 0, 0.0, pltpu.roll(u2, 1, 1))
    hc = h0
    for g in range(ng // SUBLANES):
        hin[g * SUBLANES:(g + 1) * SUBLANES, :] = a2e[g] * hc + u2e[g]
        hc = (a2[g] * hc + u2[g])[SUBLANES - 1:SUBLANES, :]
    pieces = []
    for k in range(ng):
        rows = slice(k * SUBLANES, (k + 1) * SUBLANES)
        pieces.append(sa[rows, :] * hin[k:k + 1, :] + su[rows, :])
    return jnp.concatenate(pieces, axis=0), hc


def _mixer_kernel(x_ref, g_pre_ref, w_in_ref, cw_ref, cb_ref, wa_ref, ba_ref, wx_ref, bx_ref,
                  lam_ref, lbl_ref, hgg_ref, w_a_ref, w_b_ref, w_out_ref, g_post_ref,
                  w_up32_ref, w_dn32_ref,
                  o_ref, w_up16_ref, w_dn16_ref,
                  hbuf, yabuf, ybbuf, mixbuf, sga, sgb, xcbuf, gabuf, wbuf, halo_a, hstate, sstate,
                  sa, su, hin,
                  qbuf, kbuf, gbuf, vbuf, obuf):
    tt = x_ref.shape[1]
    t_idx = pl.program_id(1)

    @pl.when(t_idx == 0)
    def _():
        halo_a[...] = jnp.zeros_like(halo_a)
        hstate[...] = jnp.zeros_like(hstate)
        sstate[...] = jnp.zeros_like(sstate)

    hbuf[...] = _rmsnorm(x_ref[0], g_pre_ref[...]).astype(BF16)

    w_up16_ref[...] = w_up32_ref[...].astype(BF16)
    w_dn16_ref[...] = w_dn32_ref[...].astype(BF16)

    def in_proj(section, c):
        return _dot(hbuf[...], w_in_ref[:, _cols(c, base=section * D_MODEL)])

    def lru_front(c):
        cc = _cols(c)
        xa = in_proj(SEC_XA, c)
        xcbuf[c % 2] = _causal_conv(xa, wbuf.at[c % 2], halo_a.at[:, cc], cw_ref[:, cc],
                                    cb_ref[:, cc], LRU_CONV)
        gabuf[c % 2] = jax.nn.gelu(in_proj(SEC_GA, c), approximate=True)

    def lru_half(c, s):
        n = c * HALVES + s
        sl = _cols(s, LANES)
        nn = _cols(n, LANES)
        xs = xcbuf[c % 2, :, sl]
        xs16 = xs.astype(BF16)
        r = jax.nn.sigmoid(_dot(xs16, wa_ref[n]) + ba_ref[:, nn])
        i = jax.nn.sigmoid(_dot(xs16, wx_ref[n]) + bx_ref[:, nn])
        nl = -lam_ref[:, nn]
        softplus = jnp.maximum(nl, 0.0) + jnp.log1p(jnp.exp(-jnp.abs(nl)))
        log_a = (-LRU_C) * r * softplus
        a = jnp.exp(log_a)
        u = jnp.sqrt(-jnp.tanh(log_a) * (a * a + 1.0)) * i * xs
        hs, h_last = _lru_scan(a, u, hstate[:, nn], sa.at[s], su.at[s], hin.at[s])
        hstate[:, nn] = h_last
        yabuf[:, nn] = (hs * gabuf[c % 2, :, sl]).astype(BF16)

    def hg_prep(c):
        pb = c % 2
        cc = _cols(c)
        q = in_proj(SEC_Q, c)
        fz = in_proj(SEC_F, c)
        v = in_proj(SEC_V, c)
        l0 = lbl_ref[0:1, cc]
        l1 = lbl_ref[1:2, cc]
        m = jnp.maximum(l0, l1)
        e0 = jnp.exp(l0 - m)
        lb = e0 / (e0 + jnp.exp(l1 - m))
        sig = jax.nn.sigmoid(fz)
        f = lb + (1.0 - lb) * sig
        k = (1.0 - lb) * (1.0 - sig)
        qbuf[pb] = q
        kbuf[pb] = k
        gbuf[pb] = _group_prefix_sum(jnp.log(f))
        vbuf[pb] = v.astype(BF16)

    causal = (lax.broadcasted_iota(jnp.int32, (HG_BLOCK, HG_BLOCK), 0)
              >= lax.broadcasted_iota(jnp.int32, (HG_BLOCK, HG_BLOCK), 1))
    subs = HG_BLOCK // HG_CHUNK

    def hg_block(c, nb):
        pb = c % 2
        rows = slice(nb * HG_BLOCK, (nb + 1) * HG_BLOCK)
        q = qbuf[pb, rows, :]
        k = kbuf[pb, rows, :]
        p = gbuf[pb, rows, :]
        v = vbuf[pb, rows, :]
        parts = [p[0:SUBLANES]]
        for j in range(1, HG_BLOCK // SUBLANES):
            parts.append(p[j * SUBLANES:(j + 1) * SUBLANES] + parts[-1][SUBLANES - 1:SUBLANES])
        g = jnp.concatenate(parts, axis=0)
        g_end = g[HG_BLOCK - 1:HG_BLOCK]
        q0 = (q * jnp.exp(g)).astype(BF16)
        kd = (k * jnp.exp(g_end - g)).astype(BF16)
        dec = jnp.exp(g_end)
        qts, kas = [], []
        for a in range(subs):
            ra = slice(a * HG_CHUNK, (a + 1) * HG_CHUNK)
            hi = (a + 1) * HG_CHUNK
            if a == 0:
                qts.append((q[ra] * jnp.exp(g[ra])).astype(BF16))
                ka = k[0:hi] * jnp.exp(-g[0:hi])
            else:
                gs = g[a * HG_CHUNK - 1:a * HG_CHUNK]
                qts.append((q[ra] * jnp.exp(g[ra] - gs)).astype(BF16))
                ka = k[0:hi] * jnp.exp(gs - g[0:hi])
            kas.append(jnp.concatenate([ka, k[hi:]], axis=0).astype(BF16) if hi < HG_BLOCK
                       else ka.astype(BF16))
        outs = []
        for s in range(HALVES):
            hd = c * HALVES + s
            sl = _cols(s, HG_DK)
            att = jnp.concatenate(
                [_dot_nt(qts[a][:, sl], kas[a][:, sl]) for a in range(subs)], axis=0)
            att = jnp.where(causal, att, 0.0).astype(BF16)
            st = sstate[hd]
            outs.append(_dot(att, v[:, sl]) + _dot_nt(q0[:, sl], st.astype(BF16)))
            sstate[hd] = st * dec[:, sl] + _dot_tn(v[:, sl], kd[:, sl])
        obuf[pb, rows, :] = jnp.concatenate(outs, axis=1)

    def hg_out(c):
        og = in_proj(SEC_OG, c)
        ob = obuf[c % 2]
        ys = []
        for s in range(HALVES):
            o = ob[:, _cols(s, HG_DK)]
            ms = jnp.mean(o * o, axis=-1, keepdims=True)
            ys.append(o * lax.rsqrt(ms + EPS))
        yb = jnp.concatenate(ys, axis=1) * hgg_ref[:, _cols(c)] * (og * jax.nn.sigmoid(og))
        ybbuf[:, _cols(c)] = yb.astype(BF16)

    def gate_a(c):
        sga[:, _cols(c)] = jax.nn.sigmoid(in_proj(SEC_GATE_A, c))

    def gate_b(c):
        sgb[:, _cols(c)] = jax.nn.sigmoid(in_proj(SEC_GATE_B, c))

    def merge(c):
        cc = _cols(c)
        za = _dot(yabuf[...], w_a_ref[:, cc])
        zb = _dot(ybbuf[...], w_b_ref[:, cc])
        mixbuf[:, cc] = (sga[:, cc] * za + sgb[:, cc] * zb).astype(BF16)

    assert tt // HG_BLOCK == 4
    for c in range(N_D_CHUNKS):
        hg_prep(c)
        lru_front(c)
        hg_block(c, 0)
        lru_half(c, 0)
        hg_block(c, 1)
        gate_a(c)
        lru_half(c, 1)
        hg_block(c, 2)
        gate_b(c)
        hg_block(c, 3)
        hg_out(c)
    for c in range(N_D_CHUNKS):
        merge(c)
    out = _dot(mixbuf[...], w_out_ref[...])
    o_ref[0] = x_ref[0] + _rmsnorm(out, g_post_ref[...])


def _ffn_kernel(x_ref, g_pre_ref, w_up_ref, cw_ref, cb_ref, w_dn_ref, g_post_ref,
                o_ref,
                hbuf, ybuf, wbuf, halo):
    t_idx = pl.program_id(1)

    @pl.when(t_idx == 0)
    def _():
        halo[...] = jnp.zeros_like(halo)

    hbuf[...] = _rmsnorm(x_ref[0], g_pre_ref[...]).astype(BF16)

    for j in range(N_FF_CHUNKS):
        h = hbuf[...]
        cg_cols = _cols(j)
        cv_cols = _cols(j, base=D_FF)
        ug = _dot(h, w_up_ref[:, cg_cols])
        uv = _dot(h, w_up_ref[:, cv_cols])
        cg = _causal_conv(ug, wbuf.at[2 * (j % 2)], halo.at[:, cg_cols], cw_ref[:, cg_cols],
                          cb_ref[:, cg_cols], FFN_CONV)
        cv = _causal_conv(uv, wbuf.at[2 * (j % 2) + 1], halo.at[:, cv_cols], cw_ref[:, cv_cols],
                          cb_ref[:, cv_cols], FFN_CONV)
        ybuf[:, cg_cols] = (jax.nn.gelu(cg, approximate=True) * cv).astype(BF16)

    o_ref[0] = x_ref[0] + _rmsnorm(_dot(ybuf[...], w_dn_ref[...]), g_post_ref[...])


def _resident(shape):
    zeros = (0,) * len(shape)
    return pl.BlockSpec(shape, lambda b, t: zeros, pipeline_mode=pl.Buffered(1))


def _row_slabs(w, n_steps):
    rows = w.shape[0] // n_steps
    assert rows * n_steps == w.shape[0] and rows % (2 * SUBLANES) == 0
    return pl.BlockSpec((rows, w.shape[1]),
                        lambda b, i: (jnp.minimum(b * n_steps + i, n_steps - 1), 0))


def _mixer_call(x, *consts, ffn_weights):
    bsz, t, d = x.shape
    tt = TIME_TILE
    n_steps = t // tt
    tile = pl.BlockSpec((1, tt, d), lambda b, i: (b, i, 0))
    slabs = [_row_slabs(w, n_steps) for w in ffn_weights]
    return pl.pallas_call(
        _mixer_kernel,
        grid=(bsz, n_steps),
        in_specs=[tile] + [_resident(c.shape) for c in consts] + slabs,
        out_specs=[tile] + slabs,
        out_shape=[jax.ShapeDtypeStruct(x.shape, x.dtype)]
        + [jax.ShapeDtypeStruct(w.shape, BF16) for w in ffn_weights],
        scratch_shapes=[
            pltpu.VMEM((tt, d), BF16),
            pltpu.VMEM((tt, d), BF16),
            pltpu.VMEM((tt, d), BF16),
            pltpu.VMEM((tt, d), BF16),
            pltpu.VMEM((tt, d), F32),
            pltpu.VMEM((tt, d), F32),
            pltpu.VMEM((2, tt, COL_CHUNK), F32),
            pltpu.VMEM((2, tt, COL_CHUNK), F32),
            pltpu.VMEM((2, tt + SUBLANES, COL_CHUNK), F32),
            pltpu.VMEM((SUBLANES, d), F32),
            pltpu.VMEM((1, d), F32),
            pltpu.VMEM((HG_HEADS, HG_DK, HG_DK), F32),
            pltpu.VMEM((HALVES, tt, LANES), F32),
            pltpu.VMEM((HALVES, tt, LANES), F32),
            pltpu.VMEM((HALVES, tt // SUBLANES, LANES), F32),
            pltpu.VMEM((2, tt, COL_CHUNK), F32),
            pltpu.VMEM((2, tt, COL_CHUNK), F32),
            pltpu.VMEM((2, tt, COL_CHUNK), F32),
            pltpu.VMEM((2, tt, COL_CHUNK), BF16),
            pltpu.VMEM((2, tt, COL_CHUNK), F32),
        ],
        compiler_params=pltpu.CompilerParams(
            dimension_semantics=("arbitrary", "arbitrary"),
            vmem_limit_bytes=VMEM_LIMIT_BYTES),
        name="mixer",
    )(x, *consts, *ffn_weights)


def _ffn_call(x, *consts):
    bsz, t, d = x.shape
    tt = TIME_TILE
    tile = pl.BlockSpec((1, tt, d), lambda b, i: (b, i, 0))
    return pl.pallas_call(
        _ffn_kernel,
        grid=(bsz, t // tt),
        in_specs=[tile] + [_resident(c.shape) for c in consts],
        out_specs=tile,
        out_shape=jax.ShapeDtypeStruct(x.shape, x.dtype),
        scratch_shapes=[
            pltpu.VMEM((tt, d), BF16),
            pltpu.VMEM((tt, D_FF), BF16),
            pltpu.VMEM((4, tt + SUBLANES, COL_CHUNK), F32),
            pltpu.VMEM((SUBLANES, 2 * D_FF), F32),
        ],
        compiler_params=pltpu.CompilerParams(
            dimension_semantics=("arbitrary", "arbitrary"),
            vmem_limit_bytes=VMEM_LIMIT_BYTES),
        name="ffn",
    )(x, *consts)


def kernel(x, norm_pre_mix, w_in, conv_a_w, conv_a_b, lru_wa, lru_ba, lru_wx, lru_bx, lru_lambda,
           hg_lb_logits, hg_norm_g, w_branch_a, w_branch_b, w_out, norm_post_mix, norm_pre_ffn,
           w_up, conv_f_w, conv_f_b, w_down, norm_post_ffn):
    depth = w_in.shape[0]
    d = D_MODEL
    assert depth == 1 and hg_lb_logits.shape[0] == 2, "single-layer configuration only"
    for l in range(depth):
        x, w_up16, w_dn16 = _mixer_call(
            x, norm_pre_mix[l].reshape(1, d), w_in[l].astype(BF16),
            conv_a_w[l], conv_a_b[l].reshape(1, d),
            lru_wa[l].astype(BF16), lru_ba[l].reshape(1, d),
            lru_wx[l].astype(BF16), lru_bx[l].reshape(1, d),
            lru_lambda[l].reshape(1, d), hg_lb_logits, hg_norm_g[l].reshape(1, d),
            w_branch_a[l].astype(BF16), w_branch_b[l].astype(BF16), w_out[l].astype(BF16),
            norm_post_mix[l].reshape(1, d), ffn_weights=(w_up[l], w_down[l]))
        x = _ffn_call(
            x, norm_pre_ffn[l].reshape(1, d), w_up16,
            conv_f_w[l], conv_f_b[l].reshape(1, 2 * D_FF),
            w_dn16, norm_post_ffn[l].reshape(1, d))
    return x
```

```python
import jax
import jax.numpy as jnp
from jax import lax
from jax.experimental import pallas as pl
from jax.experimental.pallas import tpu as pltpu

D_MODEL = 1024
LRU_BLOCKS = 8
LRU_BLOCK_W = 128
LRU_CONV = 4
LRU_C = 8.0
HG_HEADS = 8
HG_DK = 128
HG_CHUNK = 32
HG_BLOCK = 128
D_FF = 2816
FFN_CONV = 3
EPS = 1e-6

SUBLANES = 8
LANES = 128
TIME_TILE = 512
COL_CHUNK = 256
N_D_CHUNKS = D_MODEL // COL_CHUNK
N_FF_CHUNKS = D_FF // COL_CHUNK
HALVES = COL_CHUNK // LANES
VMEM_LIMIT_BYTES = 56 * 1024 * 1024

SEC_XA, SEC_GA, SEC_Q, SEC_F, SEC_V, SEC_OG, SEC_GATE_A, SEC_GATE_B = range(8)

BF16 = jnp.bfloat16
F32 = jnp.float32


def _cols(c, width=COL_CHUNK, base=0):
    return slice(base + c * width, base + (c + 1) * width)


def _rmsnorm(x, g):
    ms = jnp.mean(x * x, axis=-1, keepdims=True)
    return x * lax.rsqrt(ms + EPS) * g


def _dot(a, b):
    return jnp.dot(a, b, preferred_element_type=F32)


def _dot_nt(a, b):
    return lax.dot_general(a, b, (((1,), (1,)), ((), ())), preferred_element_type=F32)


def _dot_tn(a, b):
    return lax.dot_general(a, b, (((0,), (0,)), ((), ())), preferred_element_type=F32)


def _causal_conv(u, wbuf, halo, w, b, taps):
    t = u.shape[0]
    wbuf[0:SUBLANES, :] = halo[...]
    wbuf[SUBLANES:SUBLANES + t, :] = u
    halo[...] = u[t - SUBLANES:t, :]
    y = u * w[taps - 1:taps, :] + b
    for j in range(taps - 1):
        off = SUBLANES - (taps - 1) + j
        y = y + wbuf[off:off + t, :] * w[j:j + 1, :]
    return y


def _group_prefix_affine(a, u):
    r, c = a.shape
    a3 = a.reshape(r // SUBLANES, SUBLANES, c)
    u3 = u.reshape(r // SUBLANES, SUBLANES, c)
    row = lax.broadcasted_iota(jnp.int32, a3.shape, 1)
    for d in (1, 2, 4):
        m = row >= d
        ar = pltpu.roll(a3, d, 1)
        ur = pltpu.roll(u3, d, 1)
        u3 = jnp.where(m, a3 * ur + u3, u3)
        a3 = jnp.where(m, a3 * ar, a3)
    return a3, u3


def _group_prefix_sum(g):
    r, c = g.shape
    g3 = g.reshape(r // SUBLANES, SUBLANES, c)
    row = lax.broadcasted_iota(jnp.int32, g3.shape, 1)
    for d in (1, 2, 4):
        g3 = jnp.where(row >= d, g3 + pltpu.roll(g3, d, 1), g3)
    return g3.reshape(r, c)


def _lru_scan(a, u, h0, sa, su, hin):
    t = a.shape[0]
    ng = t // SUBLANES
    a1, u1 = _group_prefix_affine(a, u)
    sa[...] = a1.reshape(t, LANES)
    su[...] = u1.reshape(t, LANES)
    ag = sa[pl.ds(SUBLANES - 1, ng, stride=SUBLANES), :]
    ug = su[pl.ds(SUBLANES - 1, ng, stride=SUBLANES), :]
    a2, u2 = _group_prefix_affine(ag, ug)
    row = lax.broadcasted_iota(jnp.int32, a2.shape, 1)
    a2e = jnp.where(row == 0, 1.0, pltpu.roll(a2, 1, 1))
    u2e = jnp.where(row == 0, 0.0, pltpu.roll(u2, 1, 1))
    hc = h0
    for g in range(ng // SUBLANES):
        hin[g * SUBLANES:(g + 1) * SUBLANES, :] = a2e[g] * hc + u2e[g]
        hc = (a2[g] * hc + u2[g])[SUBLANES - 1:SUBLANES, :]
    pieces = []
    for k in range(ng):
        rows = slice(k * SUBLANES, (k + 1) * SUBLANES)
        pieces.append(sa[rows, :] * hin[k:k + 1, :] + su[rows, :])
    return jnp.concatenate(pieces, axis=0), hc


def _mixer_kernel(x_ref, g_pre_ref, w_in_ref, cw_ref, cb_ref, wa_ref, ba_ref, wx_ref, bx_ref,
                  lam_ref, lbl_ref, hgg_ref, w_a_ref, w_b_ref, w_out_ref, g_post_ref,
                  w_up32_ref, w_dn32_ref,
                  o_ref, w_up16_ref, w_dn16_ref,
                  hbuf, yabuf, ybbuf, mixbuf, sga, sgb, xcbuf, gabuf, wbuf, halo_a, hstate, sstate,
                  sa, su, hin,
                  qbuf, kbuf, gbuf, vbuf, obuf):
    tt = x_ref.shape[1]
    t_idx = pl.program_id(1)

    @pl.when(t_idx == 0)
    def _():
        halo_a[...] = jnp.zeros_like(halo_a)
        hstate[...] = jnp.zeros_like(hstate)
        sstate[...] = jnp.zeros_like(sstate)

    hbuf[...] = _rmsnorm(x_ref[0], g_pre_ref[...]).astype(BF16)

    w_up16_ref[...] = w_up32_ref[...].astype(BF16)
    w_dn16_ref[...] = w_dn32_ref[...].astype(BF16)

    def in_proj(section, c):
        return _dot(hbuf[...], w_in_ref[:, _cols(c, base=section * D_MODEL)])

    def lru_front(c):
        cc = _cols(c)
        xa = in_proj(SEC_XA, c)
        xcbuf[c % 2] = _causal_conv(xa, wbuf.at[c % 2], halo_a.at[:, cc], cw_ref[:, cc],
                                    cb_ref[:, cc], LRU_CONV)
        gabuf[c % 2] = jax.nn.gelu(in_proj(SEC_GA, c), approximate=True)

    def lru_half(c, s):
        n = c * HALVES + s
        sl = _cols(s, LANES)
        nn = _cols(n, LANES)
        xs = xcbuf[c % 2, :, sl]
        xs16 = xs.astype(BF16)
        r = jax.nn.sigmoid(_dot(xs16, wa_ref[n]) + ba_ref[:, nn])
        i = jax.nn.sigmoid(_dot(xs16, wx_ref[n]) + bx_ref[:, nn])
        nl = -lam_ref[:, nn]
        softplus = jnp.maximum(nl, 0.0) + jnp.log1p(jnp.exp(-jnp.abs(nl)))
        log_a = (-LRU_C) * r * softplus
        a = jnp.exp(log_a)
        z = -jnp.tanh(log_a) * (a * a + 1.0)
        u = jnp.where(z > 0.0, z * lax.rsqrt(z), 0.0) * i * xs
        hs, h_last = _lru_scan(a, u, hstate[:, nn], sa.at[s], su.at[s], hin.at[s])
        hstate[:, nn] = h_last
        yabuf[:, nn] = (hs * gabuf[c % 2, :, sl]).astype(BF16)

    def hg_prep(c):
        pb = c % 2
        cc = _cols(c)
        q = in_proj(SEC_Q, c)
        fz = in_proj(SEC_F, c)
        v = in_proj(SEC_V, c)
        l0 = lbl_ref[0:1, cc]
        l1 = lbl_ref[1:2, cc]
        m = jnp.maximum(l0, l1)
        e0 = jnp.exp(l0 - m)
        lb = e0 / (e0 + jnp.exp(l1 - m))
        sig = jax.nn.sigmoid(fz)
        f = lb + (1.0 - lb) * sig
        k = (1.0 - lb) * (1.0 - sig)
        qbuf[pb] = q
        kbuf[pb] = k
        gbuf[pb] = _group_prefix_sum(jnp.log(f))
        vbuf[pb] = v.astype(BF16)

    causal = (lax.broadcasted_iota(jnp.int32, (HG_BLOCK, HG_BLOCK), 0)
              >= lax.broadcasted_iota(jnp.int32, (HG_BLOCK, HG_BLOCK), 1))
    subs = HG_BLOCK // HG_CHUNK

    def hg_block(c, nb):
        pb = c % 2
        rows = slice(nb * HG_BLOCK, (nb + 1) * HG_BLOCK)
        q = qbuf[pb, rows, :]
        k = kbuf[pb, rows, :]
        p = gbuf[pb, rows, :]
        v = vbuf[pb, rows, :]
        parts = [p[0:SUBLANES]]
        for j in range(1, HG_BLOCK // SUBLANES):
            parts.append(p[j * SUBLANES:(j + 1) * SUBLANES] + parts[-1][SUBLANES - 1:SUBLANES])
        g = jnp.concatenate(parts, axis=0)
        g_end = g[HG_BLOCK - 1:HG_BLOCK]
        q0 = (q * jnp.exp(g)).astype(BF16)
        kd = (k * jnp.exp(g_end - g)).astype(BF16)
        dec = jnp.exp(g_end)
        qts, kas = [], []
        for a in range(subs):
            ra = slice(a * HG_CHUNK, (a + 1) * HG_CHUNK)
            hi = (a + 1) * HG_CHUNK
            if a == 0:
                qts.append((q[ra] * jnp.exp(g[ra])).astype(BF16))
                ka = k[0:hi] * jnp.exp(-g[0:hi])
            else:
                gs = g[a * HG_CHUNK - 1:a * HG_CHUNK]
                qts.append((q[ra] * jnp.exp(g[ra] - gs)).astype(BF16))
                ka = k[0:hi] * jnp.exp(gs - g[0:hi])
            kas.append(jnp.concatenate([ka, k[hi:]], axis=0).astype(BF16) if hi < HG_BLOCK
                       else ka.astype(BF16))
        outs = []
        for s in range(HALVES):
            hd = c * HALVES + s
            sl = _cols(s, HG_DK)
            att = jnp.concatenate(
                [_dot_nt(qts[a][:, sl], kas[a][:, sl]) for a in range(subs)], axis=0)
            att = jnp.where(causal, att, 0.0).astype(BF16)
            st = sstate[hd]
            outs.append(_dot(att, v[:, sl]) + _dot_nt(q0[:, sl], st.astype(BF16)))
            sstate[hd] = st * dec[:, sl] + _dot_tn(v[:, sl], kd[:, sl])
        obuf[pb, rows, :] = jnp.concatenate(outs, axis=1)

    def hg_out(c):
        og = in_proj(SEC_OG, c)
        ob = obuf[c % 2]
        ys = []
        for s in range(HALVES):
            o = ob[:, _cols(s, HG_DK)]
            ms = jnp.mean(o * o, axis=-1, keepdims=True)
            ys.append(o * lax.rsqrt(ms + EPS))
        yb = jnp.concatenate(ys, axis=1) * hgg_ref[:, _cols(c)] * (og * jax.nn.sigmoid(og))
        ybbuf[:, _cols(c)] = yb.astype(BF16)

    def gate_a(c):
        sga[:, _cols(c)] = in_proj(SEC_GATE_A, c)

    def gate_b(c):
        sgb[:, _cols(c)] = in_proj(SEC_GATE_B, c)

    def merge(c):
        cc = _cols(c)
        za = _dot(yabuf[...], w_a_ref[:, cc])
        zb = _dot(ybbuf[...], w_b_ref[:, cc])
        mixbuf[:, cc] = (jax.nn.sigmoid(sga[:, cc]) * za
                         + jax.nn.sigmoid(sgb[:, cc]) * zb).astype(BF16)

    assert tt // HG_BLOCK == 4
    for c in range(N_D_CHUNKS):
        hg_prep(c)
        lru_front(c)
        hg_block(c, 0)
        lru_half(c, 0)
        hg_block(c, 1)
        gate_a(c)
        lru_half(c, 1)
        hg_block(c, 2)
        gate_b(c)
        hg_block(c, 3)
        hg_out(c)
    for c in range(N_D_CHUNKS):
        merge(c)
    out = _dot(mixbuf[...], w_out_ref[...])
    o_ref[0] = x_ref[0] + _rmsnorm(out, g_post_ref[...])


def _ffn_kernel(x_ref, g_pre_ref, w_up_ref, cw_ref, cb_ref, w_dn_ref, g_post_ref,
                o_ref,
                hbuf, ybuf, wbuf, halo):
    t_idx = pl.program_id(1)

    @pl.when(t_idx == 0)
    def _():
        halo[...] = jnp.zeros_like(halo)

    hbuf[...] = _rmsnorm(x_ref[0], g_pre_ref[...]).astype(BF16)

    for j in range(N_FF_CHUNKS):
        h = hbuf[...]
        cg_cols = _cols(j)
        cv_cols = _cols(j, base=D_FF)
        ug = _dot(h, w_up_ref[:, cg_cols])
        uv = _dot(h, w_up_ref[:, cv_cols])
        cg = _causal_conv(ug, wbuf.at[2 * (j % 2)], halo.at[:, cg_cols], cw_ref[:, cg_cols],
                          cb_ref[:, cg_cols], FFN_CONV)
        cv = _causal_conv(uv, wbuf.at[2 * (j % 2) + 1], halo.at[:, cv_cols], cw_ref[:, cv_cols],
                          cb_ref[:, cv_cols], FFN_CONV)
        ybuf[:, cg_cols] = (jax.nn.gelu(cg, approximate=True) * cv).astype(BF16)

    o_ref[0] = x_ref[0] + _rmsnorm(_dot(ybuf[...], w_dn_ref[...]), g_post_ref[...])


def _resident(shape):
    zeros = (0,) * len(shape)
    return pl.BlockSpec(shape, lambda b, t: zeros, pipeline_mode=pl.Buffered(1))


def _row_slabs(w, n_steps):
    rows = w.shape[0] // n_steps
    assert rows * n_steps == w.shape[0] and rows % (2 * SUBLANES) == 0
    return pl.BlockSpec((rows, w.shape[1]),
                        lambda b, i: (jnp.minimum(b * n_steps + i, n_steps - 1), 0))


def _mixer_call(x, *consts, ffn_weights):
    bsz, t, d = x.shape
    tt = TIME_TILE
    n_steps = t // tt
    tile = pl.BlockSpec((1, tt, d), lambda b, i: (b, i, 0))
    slabs = [_row_slabs(w, n_steps) for w in ffn_weights]
    return pl.pallas_call(
        _mixer_kernel,
        grid=(bsz, n_steps),
        in_specs=[tile] + [_resident(c.shape) for c in consts] + slabs,
        out_specs=[tile] + slabs,
        out_shape=[jax.ShapeDtypeStruct(x.shape, x.dtype)]
        + [jax.ShapeDtypeStruct(w.shape, BF16) for w in ffn_weights],
        scratch_shapes=[
            pltpu.VMEM((tt, d), BF16),
            pltpu.VMEM((tt, d), BF16),
            pltpu.VMEM((tt, d), BF16),
            pltpu.VMEM((tt, d), BF16),
            pltpu.VMEM((tt, d), F32),
            pltpu.VMEM((tt, d), F32),
            pltpu.VMEM((2, tt, COL_CHUNK), F32),
            pltpu.VMEM((2, tt, COL_CHUNK), F32),
            pltpu.VMEM((2, tt + SUBLANES, COL_CHUNK), F32),
            pltpu.VMEM((SUBLANES, d), F32),
            pltpu.VMEM((1, d), F32),
            pltpu.VMEM((HG_HEADS, HG_DK, HG_DK), F32),
            pltpu.VMEM((HALVES, tt, LANES), F32),
            pltpu.VMEM((HALVES, tt, LANES), F32),
            pltpu.VMEM((HALVES, tt // SUBLANES, LANES), F32),
            pltpu.VMEM((2, tt, COL_CHUNK), F32),
            pltpu.VMEM((2, tt, COL_CHUNK), F32),
            pltpu.VMEM((2, tt, COL_CHUNK), F32),
            pltpu.VMEM((2, tt, COL_CHUNK), BF16),
            pltpu.VMEM((2, tt, COL_CHUNK), F32),
        ],
        compiler_params=pltpu.CompilerParams(
            dimension_semantics=("arbitrary", "arbitrary"),
            vmem_limit_bytes=VMEM_LIMIT_BYTES),
        name="mixer",
    )(x, *consts, *ffn_weights)


def _ffn_call(x, *consts):
    bsz, t, d = x.shape
    tt = TIME_TILE
    tile = pl.BlockSpec((1, tt, d), lambda b, i: (b, i, 0))
    return pl.pallas_call(
        _ffn_kernel,
        grid=(bsz, t // tt),
        in_specs=[tile] + [_resident(c.shape) for c in consts],
        out_specs=tile,
        out_shape=jax.ShapeDtypeStruct(x.shape, x.dtype),
        scratch_shapes=[
            pltpu.VMEM((tt, d), BF16),
            pltpu.VMEM((tt, D_FF), BF16),
            pltpu.VMEM((4, tt + SUBLANES, COL_CHUNK), F32),
            pltpu.VMEM((SUBLANES, 2 * D_FF), F32),
        ],
        compiler_params=pltpu.CompilerParams(
            dimension_semantics=("arbitrary", "arbitrary"),
            vmem_limit_bytes=VMEM_LIMIT_BYTES),
        name="ffn",
    )(x, *consts)


def kernel(x, norm_pre_mix, w_in, conv_a_w, conv_a_b, lru_wa, lru_ba, lru_wx, lru_bx, lru_lambda,
           hg_lb_logits, hg_norm_g, w_branch_a, w_branch_b, w_out, norm_post_mix, norm_pre_ffn,
           w_up, conv_f_w, conv_f_b, w_down, norm_post_ffn):
    depth = w_in.shape[0]
    d = D_MODEL
    assert depth == 1 and hg_lb_logits.shape[0] == 2, "single-layer configuration only"
    for l in range(depth):
        x, w_up16, w_dn16 = _mixer_call(
            x, norm_pre_mix[l].reshape(1, d), w_in[l].astype(BF16),
            conv_a_w[l], conv_a_b[l].reshape(1, d),
            lru_wa[l].astype(BF16), lru_ba[l].reshape(1, d),
            lru_wx[l].astype(BF16), lru_bx[l].reshape(1, d),
            lru_lambda[l].reshape(1, d), hg_lb_logits, hg_norm_g[l].reshape(1, d),
            w_branch_a[l].astype(BF16), w_branch_b[l].astype(BF16), w_out[l].astype(BF16),
            norm_post_mix[l].reshape(1, d), ffn_weights=(w_up[l], w_down[l]))
        x = _ffn_call(
            x, norm_pre_ffn[l].reshape(1, d), w_up16,
            conv_f_w[l], conv_f_b[l].reshape(1, 2 * D_FF),
            w_dn16, norm_post_ffn[l].reshape(1, d))
    return x
```

```python
import jax
import jax.numpy as jnp
from jax import lax
from jax.experimental import pallas as pl
from jax.experimental.pallas import tpu as pltpu

D_MODEL = 1024
LRU_BLOCKS = 8
LRU_BLOCK_W = 128
LRU_CONV = 4
LRU_C = 8.0
HG_HEADS = 8
HG_DK = 128
HG_CHUNK = 32
HG_BLOCK = 128
D_FF = 2816
FFN_CONV = 3
EPS = 1e-6

SUBLANES = 8
LANES = 128
TIME_TILE = 512
COL_CHUNK = 256
N_D_CHUNKS = D_MODEL // COL_CHUNK
N_FF_CHUNKS = D_FF // COL_CHUNK
HALVES = COL_CHUNK // LANES
VMEM_LIMIT_BYTES = 56 * 1024 * 1024

SEC_XA, SEC_GA, SEC_Q, SEC_F, SEC_V, SEC_OG, SEC_GATE_A, SEC_GATE_B = range(8)

BF16 = jnp.bfloat16
F32 = jnp.float32


def _cols(c, width=COL_CHUNK, base=0):
    return slice(base + c * width, base + (c + 1) * width)


def _rmsnorm(x, g):
    ms = jnp.mean(x * x, axis=-1, keepdims=True)
    return x * lax.rsqrt(ms + EPS) * g


def _dot(a, b):
    return jnp.dot(a, b, preferred_element_type=F32)


def _dot_nt(a, b):
    return lax.dot_general(a, b, (((1,), (1,)), ((), ())), preferred_element_type=F32)


def _dot_tn(a, b):
    return lax.dot_general(a, b, (((0,), (0,)), ((), ())), preferred_element_type=F32)


def _causal_conv(u, wbuf, halo, w, b, taps):
    t = u.shape[0]
    wbuf[0:SUBLANES, :] = halo[...]
    wbuf[SUBLANES:SUBLANES + t, :] = u
    halo[...] = u[t - SUBLANES:t, :]
    y = u * w[taps - 1:taps, :] + b
    for j in range(taps - 1):
        off = SUBLANES - (taps - 1) + j
        y = y + wbuf[off:off + t, :] * w[j:j + 1, :]
    return y


def _group_prefix_affine(a, u):
    r, c = a.shape
    a3 = a.reshape(r // SUBLANES, SUBLANES, c)
    u3 = u.reshape(r // SUBLANES, SUBLANES, c)
    row = lax.broadcasted_iota(jnp.int32, a3.shape, 1)
    for d in (1, 2, 4):
        m = row >= d
        ar = pltpu.roll(a3, d, 1)
        ur = pltpu.roll(u3, d, 1)
        u3 = jnp.where(m, a3 * ur + u3, u3)
        a3 = jnp.where(m, a3 * ar, a3)
    return a3, u3


def _group_prefix_sum(g):
    r, c = g.shape
    g3 = g.reshape(r // SUBLANES, SUBLANES, c)
    row = lax.broadcasted_iota(jnp.int32, g3.shape, 1)
    for d in (1, 2, 4):
        g3 = jnp.where(row >= d, g3 + pltpu.roll(g3, d, 1), g3)
    return g3.reshape(r, c)


def _lru_scan(a, u, h0, sa, su, hin):
    t = a.shape[0]
    ng = t // SUBLANES
    a1, u1 = _group_prefix_affine(a, u)
    sa[...] = a1.reshape(t, LANES)
    su[...] = u1.reshape(t, LANES)
    ag = sa[pl.ds(SUBLANES - 1, ng, stride=SUBLANES), :]
    ug = su[pl.ds(SUBLANES - 1, ng, stride=SUBLANES), :]
    a2, u2 = _group_prefix_affine(ag, ug)
    row = lax.broadcasted_iota(jnp.int32, a2.shape, 1)
    a2e = jnp.where(row == 0, 1.0, pltpu.roll(a2, 1, 1))
    u2e = jnp.where(row == 0, 0.0, pltpu.roll(u2, 1, 1))
    hc = h0
    for g in range(ng // SUBLANES):
        hin[g * SUBLANES:(g + 1) * SUBLANES, :] = a2e[g] * hc + u2e[g]
        hc = (a2[g] * hc + u2[g])[SUBLANES - 1:SUBLANES, :]
    pieces = []
    for k in range(ng):
        rows = slice(k * SUBLANES, (k + 1) * SUBLANES)
        pieces.append(sa[rows, :] * hin[k:k + 1, :] + su[rows, :])
    return jnp.concatenate(pieces, axis=0), hc


def _mixer_kernel(x_ref, g_pre_ref, w_in_ref, cw_ref, cb_ref, wa_ref, ba_ref, wx_ref, bx_ref,
                  lam_ref, lbl_ref, hgg_ref, w_a_ref, w_b_ref, w_out_ref, g_post_ref,
                  w_up32_ref, w_dn32_ref,
                  o_ref, w_up16_ref, w_dn16_ref,
                  hbuf, yabuf, ybbuf, mixbuf, sga, sgb, xcbuf, gabuf, wbuf, halo_a, hstate, sstate,
                  sa, su, hin,
                  qbuf, kbuf, gbuf, vbuf, obuf):
    tt = x_ref.shape[1]
    t_idx = pl.program_id(1)

    @pl.when(t_idx == 0)
    def _():
        halo_a[...] = jnp.zeros_like(halo_a)
        hstate[...] = jnp.zeros_like(hstate)
        sstate[...] = jnp.zeros_like(sstate)

    hbuf[...] = _rmsnorm(x_ref[0], g_pre_ref[...]).astype(BF16)

    w_up16_ref[...] = w_up32_ref[...].astype(BF16)
    w_dn16_ref[...] = w_dn32_ref[...].astype(BF16)

    def in_proj(section, c):
        return _dot(hbuf[...], w_in_ref[:, _cols(c, base=section * D_MODEL)])

    def lru_front(c):
        cc = _cols(c)
        xa = in_proj(SEC_XA, c)
        xcbuf[c % 2] = _causal_conv(xa, wbuf.at[c % 2], halo_a.at[:, cc], cw_ref[:, cc],
                                    cb_ref[:, cc], LRU_CONV)
        gabuf[c % 2] = jax.nn.gelu(in_proj(SEC_GA, c), approximate=True)

    def lru_half(c, s):
        n = c * HALVES + s
        sl = _cols(s, LANES)
        nn = _cols(n, LANES)
        xs = xcbuf[c % 2, :, sl]
        xs16 = xs.astype(BF16)
        r = jax.nn.sigmoid(_dot(xs16, wa_ref[n]) + ba_ref[:, nn])
        i = jax.nn.sigmoid(_dot(xs16, wx_ref[n]) + bx_ref[:, nn])
        nl = -lam_ref[:, nn]
        softplus = jnp.maximum(nl, 0.0) + jnp.log1p(jnp.exp(-jnp.abs(nl)))
        log_a = (-LRU_C) * r * softplus
        a = jnp.exp(log_a)
        z = -jnp.tanh(log_a) * (a * a + 1.0)
        u = jnp.where(z > 0.0, z * lax.rsqrt(z), 0.0) * i * xs
        hs, h_last = _lru_scan(a, u, hstate[:, nn], sa.at[s], su.at[s], hin.at[s])
        hstate[:, nn] = h_last
        yabuf[:, nn] = (hs * gabuf[c % 2, :, sl]).astype(BF16)

    def hg_prep(c):
        pb = c % 2
        cc = _cols(c)
        q = in_proj(SEC_Q, c)
        fz = in_proj(SEC_F, c)
        v = in_proj(SEC_V, c)
        l0 = lbl_ref[0:1, cc]
        l1 = lbl_ref[1:2, cc]
        m = jnp.maximum(l0, l1)
        e0 = jnp.exp(l0 - m)
        lb = e0 / (e0 + jnp.exp(l1 - m))
        sig = jax.nn.sigmoid(fz)
        f = lb + (1.0 - lb) * sig
        k = (1.0 - lb) * (1.0 - sig)
        qbuf[pb] = q
        kbuf[pb] = k
        gbuf[pb] = _group_prefix_sum(jnp.log(f))
        vbuf[pb] = v.astype(BF16)

    causal = (lax.broadcasted_iota(jnp.int32, (HG_BLOCK, HG_BLOCK), 0)
              >= lax.broadcasted_iota(jnp.int32, (HG_BLOCK, HG_BLOCK), 1))
    subs = HG_BLOCK // HG_CHUNK

    def hg_block(c, nb):
        pb = c % 2
        rows = slice(nb * HG_BLOCK, (nb + 1) * HG_BLOCK)
        q = qbuf[pb, rows, :]
        k = kbuf[pb, rows, :]
        p = gbuf[pb, rows, :]
        v = vbuf[pb, rows, :]
        parts = [p[0:SUBLANES]]
        for j in range(1, HG_BLOCK // SUBLANES):
            parts.append(p[j * SUBLANES:(j + 1) * SUBLANES] + parts[-1][SUBLANES - 1:SUBLANES])
        g = jnp.concatenate(parts, axis=0)
        g_end = g[HG_BLOCK - 1:HG_BLOCK]
        q0 = (q * jnp.exp(g)).astype(BF16)
        kd = (k * jnp.exp(g_end - g)).astype(BF16)
        dec = jnp.exp(g_end)
        qts, kas = [], []
        for a in range(subs):
            ra = slice(a * HG_CHUNK, (a + 1) * HG_CHUNK)
            hi = (a + 1) * HG_CHUNK
            if a == 0:
                qts.append((q[ra] * jnp.exp(g[ra])).astype(BF16))
                ka = k[0:hi] * jnp.exp(-g[0:hi])
            else:
                gs = g[a * HG_CHUNK - 1:a * HG_CHUNK]
                qts.append((q[ra] * jnp.exp(g[ra] - gs)).astype(BF16))
                ka = k[0:hi] * jnp.exp(gs - g[0:hi])
            kas.append(jnp.concatenate([ka, k[hi:]], axis=0).astype(BF16) if hi < HG_BLOCK
                       else ka.astype(BF16))
        outs = []
        for s in range(HALVES):
            hd = c * HALVES + s
            sl = _cols(s, HG_DK)
            att = jnp.concatenate(
                [_dot_nt(qts[a][:, sl], kas[a][:, sl]) for a in range(subs)], axis=0)
            att = jnp.where(causal, att, 0.0).astype(BF16)
            st = sstate[hd]
            outs.append(_dot(att, v[:, sl]) + _dot_nt(q0[:, sl], st.astype(BF16)))
            sstate[hd] = st * dec[:, sl] + _dot_tn(v[:, sl], kd[:, sl])
        obuf[pb, rows, :] = jnp.concatenate(outs, axis=1)

    def hg_out(c):
        og = in_proj(SEC_OG, c)
        ob = obuf[c % 2]
        ys = []
        for s in range(HALVES):
            o = ob[:, _cols(s, HG_DK)]
            ms = jnp.mean(o * o, axis=-1, keepdims=True)
            ys.append(o * lax.rsqrt(ms + EPS))
        yb = jnp.concatenate(ys, axis=1) * hgg_ref[:, _cols(c)] * (og * jax.nn.sigmoid(og))
        ybbuf[:, _cols(c)] = yb.astype(BF16)

    def gate_a(c):
        sga[:, _cols(c)] = in_proj(SEC_GATE_A, c)

    def gate_b(c):
        sgb[:, _cols(c)] = in_proj(SEC_GATE_B, c)

    def merge(c):
        cc = _cols(c)
        za = _dot(yabuf[...], w_a_ref[:, cc])
        zb = _dot(ybbuf[...], w_b_ref[:, cc])
        mixbuf[:, cc] = (jax.nn.sigmoid(sga[:, cc]) * za
                         + jax.nn.sigmoid(sgb[:, cc]) * zb).astype(BF16)

    assert tt // HG_BLOCK == 4
    for c in range(N_D_CHUNKS):
        hg_prep(c)
        lru_front(c)
        hg_block(c, 0)
        lru_half(c, 0)
        hg_block(c, 1)
        gate_a(c)
        lru_half(c, 1)
        hg_block(c, 2)
        gate_b(c)
        hg_block(c, 3)
        hg_out(c)
    for c in range(N_D_CHUNKS):
        merge(c)
    out = _dot(mixbuf[...], w_out_ref[...])
    o_ref[0] = x_ref[0] + _rmsnorm(out, g_post_ref[...])


def _ffn_kernel(x_ref, g_pre_ref, w_up_ref, cw_ref, cb_ref, w_dn_ref, g_post_ref,
                o_ref,
                hbuf, cgbuf, cvbuf, wbuf, halo):
    t_idx = pl.program_id(1)

    @pl.when(t_idx == 0)
    def _():
        halo[...] = jnp.zeros_like(halo)

    hbuf[...] = _rmsnorm(x_ref[0], g_pre_ref[...]).astype(BF16)

    for j in range(N_FF_CHUNKS):
        h = hbuf[...]
        cg_cols = _cols(j)
        cv_cols = _cols(j, base=D_FF)
        ug = _dot(h, w_up_ref[:, cg_cols])
        uv = _dot(h, w_up_ref[:, cv_cols])
        cg = _causal_conv(ug, wbuf.at[2 * (j % 2)], halo.at[:, cg_cols], cw_ref[:, cg_cols],
                          cb_ref[:, cg_cols], FFN_CONV)
        cv = _causal_conv(uv, wbuf.at[2 * (j % 2) + 1], halo.at[:, cv_cols], cw_ref[:, cv_cols],
                          cb_ref[:, cv_cols], FFN_CONV)
        cgbuf[:, cg_cols] = cg
        cvbuf[:, cg_cols] = cv

    acc = None
    for j in range(N_FF_CHUNKS):
        cols = _cols(j)
        y = (jax.nn.gelu(cgbuf[:, cols], approximate=True) * cvbuf[:, cols]).astype(BF16)
        part = _dot(y, w_dn_ref[cols, :])
        acc = part if acc is None else acc + part
    o_ref[0] = x_ref[0] + _rmsnorm(acc, g_post_ref[...])


def _resident(shape):
    zeros = (0,) * len(shape)
    return pl.BlockSpec(shape, lambda b, t: zeros, pipeline_mode=pl.Buffered(1))


def _row_slabs(w, n_steps):
    rows = w.shape[0] // n_steps
    assert rows * n_steps == w.shape[0] and rows % (2 * SUBLANES) == 0
    return pl.BlockSpec((rows, w.shape[1]),
                        lambda b, i: (jnp.minimum(b * n_steps + i, n_steps - 1), 0))


def _mixer_call(x, *consts, ffn_weights):
    bsz, t, d = x.shape
    tt = TIME_TILE
    n_steps = t // tt
    tile = pl.BlockSpec((1, tt, d), lambda b, i: (b, i, 0))
    slabs = [_row_slabs(w, n_steps) for w in ffn_weights]
    return pl.pallas_call(
        _mixer_kernel,
        grid=(bsz, n_steps),
        in_specs=[tile] + [_resident(c.shape) for c in consts] + slabs,
        out_specs=[tile] + slabs,
        out_shape=[jax.ShapeDtypeStruct(x.shape, x.dtype)]
        + [jax.ShapeDtypeStruct(w.shape, BF16) for w in ffn_weights],
        scratch_shapes=[
            pltpu.VMEM((tt, d), BF16),
            pltpu.VMEM((tt, d), BF16),
            pltpu.VMEM((tt, d), BF16),
            pltpu.VMEM((tt, d), BF16),
            pltpu.VMEM((tt, d), F32),
            pltpu.VMEM((tt, d), F32),
            pltpu.VMEM((2, tt, COL_CHUNK), F32),
            pltpu.VMEM((2, tt, COL_CHUNK), F32),
            pltpu.VMEM((2, tt + SUBLANES, COL_CHUNK), F32),
            pltpu.VMEM((SUBLANES, d), F32),
            pltpu.VMEM((1, d), F32),
            pltpu.VMEM((HG_HEADS, HG_DK, HG_DK), F32),
            pltpu.VMEM((HALVES, tt, LANES), F32),
            pltpu.VMEM((HALVES, tt, LANES), F32),
            pltpu.VMEM((HALVES, tt // SUBLANES, LANES), F32),
            pltpu.VMEM((2, tt, COL_CHUNK), F32),
            pltpu.VMEM((2, tt, COL_CHUNK), F32),
            pltpu.VMEM((2, tt, COL_CHUNK), F32),
            pltpu.VMEM((2, tt, COL_CHUNK), BF16),
            pltpu.VMEM((2, tt, COL_CHUNK), F32),
        ],
        compiler_params=pltpu.CompilerParams(
            dimension_semantics=("arbitrary", "arbitrary"),
            vmem_limit_bytes=VMEM_LIMIT_BYTES),
        name="mixer",
    )(x, *consts, *ffn_weights)


def _ffn_call(x, *consts):
    bsz, t, d = x.shape
    tt = TIME_TILE
    tile = pl.BlockSpec((1, tt, d), lambda b, i: (b, i, 0))
    return pl.pallas_call(
        _ffn_kernel,
        grid=(bsz, t // tt),
        in_specs=[tile] + [_resident(c.shape) for c in consts],
        out_specs=tile,
        out_shape=jax.ShapeDtypeStruct(x.shape, x.dtype),
        scratch_shapes=[
            pltpu.VMEM((tt, d), BF16),
            pltpu.VMEM((tt, D_FF), F32),
            pltpu.VMEM((tt, D_FF), F32),
            pltpu.VMEM((4, tt + SUBLANES, COL_CHUNK), F32),
            pltpu.VMEM((SUBLANES, 2 * D_FF), F32),
        ],
        compiler_params=pltpu.CompilerParams(
            dimension_semantics=("arbitrary", "arbitrary"),
            vmem_limit_bytes=VMEM_LIMIT_BYTES),
        name="ffn",
    )(x, *consts)


def kernel(x, norm_pre_mix, w_in, conv_a_w, conv_a_b, lru_wa, lru_ba, lru_wx, lru_bx, lru_lambda,
           hg_lb_logits, hg_norm_g, w_branch_a, w_branch_b, w_out, norm_post_mix, norm_pre_ffn,
           w_up, conv_f_w, conv_f_b, w_down, norm_post_ffn):
    depth = w_in.shape[0]
    d = D_MODEL
    assert depth == 1 and hg_lb_logits.shape[0] == 2, "single-layer configuration only"
    for l in range(depth):
        x, w_up16, w_dn16 = _mixer_call(
            x, norm_pre_mix[l].reshape(1, d), w_in[l].astype(BF16),
            conv_a_w[l], conv_a_b[l].reshape(1, d),
            lru_wa[l].astype(BF16), lru_ba[l].reshape(1, d),
            lru_wx[l].astype(BF16), lru_bx[l].reshape(1, d),
            lru_lambda[l].reshape(1, d), hg_lb_logits, hg_norm_g[l].reshape(1, d),
            w_branch_a[l].astype(BF16), w_branch_b[l].astype(BF16), w_out[l].astype(BF16),
            norm_post_mix[l].reshape(1, d), ffn_weights=(w_up[l], w_down[l]))
        x = _ffn_call(
            x, norm_pre_ffn[l].reshape(1, d), w_up16,
            conv_f_w[l], conv_f_b[l].reshape(1, 2 * D_FF),
            w_dn16, norm_post_ffn[l].reshape(1, d))
    return x
```

```python
import jax
import jax.numpy as jnp
from jax import lax
from jax.experimental import pallas as pl
from jax.experimental.pallas import tpu as pltpu

D_MODEL = 1024
LRU_BLOCKS = 8
LRU_BLOCK_W = 128
LRU_CONV = 4
LRU_C = 8.0
HG_HEADS = 8
HG_DK = 128
HG_CHUNK = 32
HG_BLOCK = 128
D_FF = 2816
FFN_CONV = 3
EPS = 1e-6

SUBLANES = 8
LANES = 128
TIME_TILE = 512
COL_CHUNK = 256
N_D_CHUNKS = D_MODEL // COL_CHUNK
N_FF_CHUNKS = D_FF // COL_CHUNK
HALVES = COL_CHUNK // LANES
VMEM_LIMIT_BYTES = 60 * 1024 * 1024

SEC_XA, SEC_GA, SEC_Q, SEC_F, SEC_V, SEC_OG, SEC_GATE_A, SEC_GATE_B = range(8)

BF16 = jnp.bfloat16
F32 = jnp.float32


def _cols(c, width=COL_CHUNK, base=0):
    return slice(base + c * width, base + (c + 1) * width)


def _rmsnorm(x, g):
    ms = jnp.mean(x * x, axis=-1, keepdims=True)
    return x * lax.rsqrt(ms + EPS) * g


def _dot(a, b):
    return jnp.dot(a, b, preferred_element_type=F32)


def _dot_nt(a, b):
    return lax.dot_general(a, b, (((1,), (1,)), ((), ())), preferred_element_type=F32)


def _dot_tn(a, b):
    return lax.dot_general(a, b, (((0,), (0,)), ((), ())), preferred_element_type=F32)


def _causal_conv(u, wbuf, halo, w, b, taps):
    t = u.shape[0]
    wbuf[0:SUBLANES, :] = halo[...]
    wbuf[SUBLANES:SUBLANES + t, :] = u
    halo[...] = u[t - SUBLANES:t, :]
    y = u * w[taps - 1:taps, :] + b
    for j in range(taps - 1):
        off = SUBLANES - (taps - 1) + j
        y = y + wbuf[off:off + t, :] * w[j:j + 1, :]
    return y


def _group_prefix_affine(a, u):
    r, c = a.shape
    a3 = a.reshape(r // SUBLANES, SUBLANES, c)
    u3 = u.reshape(r // SUBLANES, SUBLANES, c)
    row = lax.broadcasted_iota(jnp.int32, a3.shape, 1)
    for d in (1, 2, 4):
        m = row >= d
        ar = pltpu.roll(a3, d, 1)
        ur = pltpu.roll(u3, d, 1)
        u3 = jnp.where(m, a3 * ur + u3, u3)
        a3 = jnp.where(m, a3 * ar, a3)
    return a3, u3


def _group_prefix_sum(g):
    r, c = g.shape
    g3 = g.reshape(r // SUBLANES, SUBLANES, c)
    row = lax.broadcasted_iota(jnp.int32, g3.shape, 1)
    for d in (1, 2, 4):
        g3 = jnp.where(row >= d, g3 + pltpu.roll(g3, d, 1), g3)
    return g3.reshape(r, c)


def _lru_scan(a, u, h0, sa, su, hin):
    t = a.shape[0]
    ng = t // SUBLANES
    a1, u1 = _group_prefix_affine(a, u)
    sa[...] = a1.reshape(t, LANES)
    su[...] = u1.reshape(t, LANES)
    ag = sa[pl.ds(SUBLANES - 1, ng, stride=SUBLANES), :]
    ug = su[pl.ds(SUBLANES - 1, ng, stride=SUBLANES), :]
    a2, u2 = _group_prefix_affine(ag, ug)
    row = lax.broadcasted_iota(jnp.int32, a2.shape, 1)
    a2e = jnp.where(row == 0, 1.0, pltpu.roll(a2, 1, 1))
    u2e = jnp.where(row == 0, 0.0, pltpu.roll(u2, 1, 1))
    hc = h0
    for g in range(ng // SUBLANES):
        hin[g * SUBLANES:(g + 1) * SUBLANES, :] = a2e[g] * hc + u2e[g]
        hc = (a2[g] * hc + u2[g])[SUBLANES - 1:SUBLANES, :]
    pieces = []
    for k in range(ng):
        rows = slice(k * SUBLANES, (k + 1) * SUBLANES)
        pieces.append(sa[rows, :] * hin[k:k + 1, :] + su[rows, :])
    return jnp.concatenate(pieces, axis=0), hc


def _mixer_kernel(x_ref, g_pre_ref, w_in_ref, cw_ref, cb_ref, wa_ref, ba_ref, wx_ref, bx_ref,
                  lam_ref, lbl_ref, hgg_ref, w_a_ref, w_b_ref, w_out_ref, g_post_ref,
                  w_up32_ref, w_dn32_ref,
                  o_ref, w_up16_ref, w_dn16_ref,
                  hbuf, yabuf, ybbuf, mixbuf, sga, sgb, xcbuf, gabuf, wbuf, halo_a, hstate, sstate,
                  sa, su, hin,
                  qbuf, kbuf, gbuf, vbuf, obuf):
    tt = x_ref.shape[1]
    t_idx = pl.program_id(1)

    @pl.when(t_idx == 0)
    def _():
        halo_a[...] = jnp.zeros_like(halo_a)
        hstate[...] = jnp.zeros_like(hstate)
        sstate[...] = jnp.zeros_like(sstate)

    hbuf[...] = _rmsnorm(x_ref[0], g_pre_ref[...]).astype(BF16)

    w_up16_ref[...] = w_up32_ref[...].astype(BF16)
    w_dn16_ref[...] = w_dn32_ref[...].astype(BF16)

    def in_proj(section, c):
        return _dot(hbuf[...], w_in_ref[:, _cols(c, base=section * D_MODEL)])

    def lru_front(c):
        cc = _cols(c)
        xa = in_proj(SEC_XA, c)
        xcbuf[c % 2] = _causal_conv(xa, wbuf.at[c % 2], halo_a.at[:, cc], cw_ref[:, cc],
                                    cb_ref[:, cc], LRU_CONV)
        gabuf[c % 2] = jax.nn.gelu(in_proj(SEC_GA, c), approximate=True)

    def lru_half(c, s):
        n = c * HALVES + s
        sl = _cols(s, LANES)
        nn = _cols(n, LANES)
        xs = xcbuf[c % 2, :, sl]
        xs16 = xs.astype(BF16)
        r = jax.nn.sigmoid(_dot(xs16, wa_ref[n]) + ba_ref[:, nn])
        i = jax.nn.sigmoid(_dot(xs16, wx_ref[n]) + bx_ref[:, nn])
        nl = -lam_ref[:, nn]
        softplus = jnp.maximum(nl, 0.0) + jnp.log1p(jnp.exp(-jnp.abs(nl)))
        log_a = (-LRU_C) * r * softplus
        a = jnp.exp(log_a)
        z = -jnp.tanh(log_a) * (a * a + 1.0)
        u = jnp.where(z > 0.0, z * lax.rsqrt(z), 0.0) * i * xs
        hs, h_last = _lru_scan(a, u, hstate[:, nn], sa.at[s], su.at[s], hin.at[s])
        hstate[:, nn] = h_last
        yabuf[:, nn] = (hs * gabuf[c % 2, :, sl]).astype(BF16)

    def hg_prep(c):
        pb = c
        cc = _cols(c)
        q = in_proj(SEC_Q, c)
        fz = in_proj(SEC_F, c)
        v = in_proj(SEC_V, c)
        l0 = lbl_ref[0:1, cc]
        l1 = lbl_ref[1:2, cc]
        m = jnp.maximum(l0, l1)
        e0 = jnp.exp(l0 - m)
        lb = e0 / (e0 + jnp.exp(l1 - m))
        sig = jax.nn.sigmoid(fz)
        f = lb + (1.0 - lb) * sig
        k = (1.0 - lb) * (1.0 - sig)
        qbuf[pb] = q
        kbuf[pb] = k
        gbuf[pb] = _group_prefix_sum(jnp.log(f))
        vbuf[pb] = v.astype(BF16)

    causal = (lax.broadcasted_iota(jnp.int32, (HG_BLOCK, HG_BLOCK), 0)
              >= lax.broadcasted_iota(jnp.int32, (HG_BLOCK, HG_BLOCK), 1))
    subs = HG_BLOCK // HG_CHUNK

    def hg_block(c, nb):
        pb = c
        rows = slice(nb * HG_BLOCK, (nb + 1) * HG_BLOCK)
        q = qbuf[pb, rows, :]
        k = kbuf[pb, rows, :]
        p = gbuf[pb, rows, :]
        v = vbuf[pb, rows, :]
        parts = [p[0:SUBLANES]]
        for j in range(1, HG_BLOCK // SUBLANES):
            parts.append(p[j * SUBLANES:(j + 1) * SUBLANES] + parts[-1][SUBLANES - 1:SUBLANES])
        g = jnp.concatenate(parts, axis=0)
        g_end = g[HG_BLOCK - 1:HG_BLOCK]
        q0 = (q * jnp.exp(g)).astype(BF16)
        kd = (k * jnp.exp(g_end - g)).astype(BF16)
        dec = jnp.exp(g_end)
        qts, kas = [], []
        for a in range(subs):
            ra = slice(a * HG_CHUNK, (a + 1) * HG_CHUNK)
            hi = (a + 1) * HG_CHUNK
            if a == 0:
                qts.append((q[ra] * jnp.exp(g[ra])).astype(BF16))
                ka = k[0:hi] * jnp.exp(-g[0:hi])
            else:
                gs = g[a * HG_CHUNK - 1:a * HG_CHUNK]
                qts.append((q[ra] * jnp.exp(g[ra] - gs)).astype(BF16))
                ka = k[0:hi] * jnp.exp(gs - g[0:hi])
            kas.append(jnp.concatenate([ka, k[hi:]], axis=0).astype(BF16) if hi < HG_BLOCK
                       else ka.astype(BF16))
        outs = []
        for s in range(HALVES):
            hd = c * HALVES + s
            sl = _cols(s, HG_DK)
            att = jnp.concatenate(
                [_dot_nt(qts[a][:, sl], kas[a][:, sl]) for a in range(subs)], axis=0)
            att = jnp.where(causal, att, 0.0).astype(BF16)
            st = sstate[hd]
            outs.append(_dot(att, v[:, sl]) + _dot_nt(q0[:, sl], st.astype(BF16)))
            sstate[hd] = st * dec[:, sl] + _dot_tn(v[:, sl], kd[:, sl])
        obuf[pb, rows, :] = jnp.concatenate(outs, axis=1)

    def hg_out(c):
        og = in_proj(SEC_OG, c)
        ob = obuf[c]
        ys = []
        for s in range(HALVES):
            o = ob[:, _cols(s, HG_DK)]
            ms = jnp.mean(o * o, axis=-1, keepdims=True)
            ys.append(o * lax.rsqrt(ms + EPS))
        yb = jnp.concatenate(ys, axis=1) * hgg_ref[:, _cols(c)] * (og * jax.nn.sigmoid(og))
        ybbuf[:, _cols(c)] = yb.astype(BF16)

    def gate_a(c):
        sga[:, _cols(c)] = in_proj(SEC_GATE_A, c)

    def gate_b(c):
        sgb[:, _cols(c)] = in_proj(SEC_GATE_B, c)

    def merge(c):
        cc = _cols(c)
        za = _dot(yabuf[...], w_a_ref[:, cc])
        zb = _dot(ybbuf[...], w_b_ref[:, cc])
        mixbuf[:, cc] = (jax.nn.sigmoid(sga[:, cc]) * za
                         + jax.nn.sigmoid(sgb[:, cc]) * zb).astype(BF16)

    assert tt // HG_BLOCK == N_D_CHUNKS
    for c in range(N_D_CHUNKS):
        hg_prep(c)
    for c in range(N_D_CHUNKS):
        lru_front(c)
        lru_half(c, 0)
        gate_a(c)
        for cc in range(N_D_CHUNKS):
            hg_block(cc, c)
        lru_half(c, 1)
        gate_b(c)
    for c in range(N_D_CHUNKS):
        hg_out(c)
    for c in range(N_D_CHUNKS):
        merge(c)
    out = _dot(mixbuf[...], w_out_ref[...])
    o_ref[0] = x_ref[0] + _rmsnorm(out, g_post_ref[...])


def _ffn_kernel(x_ref, g_pre_ref, w_up_ref, cw_ref, cb_ref, w_dn_ref, g_post_ref,
                o_ref,
                hbuf, cgbuf, cvbuf, wbuf, halo):
    t_idx = pl.program_id(1)

    @pl.when(t_idx == 0)
    def _():
        halo[...] = jnp.zeros_like(halo)

    hbuf[...] = _rmsnorm(x_ref[0], g_pre_ref[...]).astype(BF16)

    for j in range(N_FF_CHUNKS):
        h = hbuf[...]
        cg_cols = _cols(j)
        cv_cols = _cols(j, base=D_FF)
        ug = _dot(h, w_up_ref[:, cg_cols])
        uv = _dot(h, w_up_ref[:, cv_cols])
        cg = _causal_conv(ug, wbuf.at[2 * (j % 2)], halo.at[:, cg_cols], cw_ref[:, cg_cols],
                          cb_ref[:, cg_cols], FFN_CONV)
        cv = _causal_conv(uv, wbuf.at[2 * (j % 2) + 1], halo.at[:, cv_cols], cw_ref[:, cv_cols],
                          cb_ref[:, cv_cols], FFN_CONV)
        cgbuf[:, cg_cols] = cg
        cvbuf[:, cg_cols] = cv

    acc = None
    for j in range(N_FF_CHUNKS):
        cols = _cols(j)
        y = (jax.nn.gelu(cgbuf[:, cols], approximate=True) * cvbuf[:, cols]).astype(BF16)
        part = _dot(y, w_dn_ref[cols, :])
        acc = part if acc is None else acc + part
    o_ref[0] = x_ref[0] + _rmsnorm(acc, g_post_ref[...])


def _resident(shape):
    zeros = (0,) * len(shape)
    return pl.BlockSpec(shape, lambda b, t: zeros, pipeline_mode=pl.Buffered(1))


def _row_slabs(w, n_steps):
    rows = w.shape[0] // n_steps
    assert rows * n_steps == w.shape[0] and rows % (2 * SUBLANES) == 0
    return pl.BlockSpec((rows, w.shape[1]),
                        lambda b, i: (jnp.minimum(b * n_steps + i, n_steps - 1), 0))


def _mixer_call(x, *consts, ffn_weights):
    bsz, t, d = x.shape
    tt = TIME_TILE
    n_steps = t // tt
    tile = pl.BlockSpec((1, tt, d), lambda b, i: (b, i, 0))
    slabs = [_row_slabs(w, n_steps) for w in ffn_weights]
    return pl.pallas_call(
        _mixer_kernel,
        grid=(bsz, n_steps),
        in_specs=[tile] + [_resident(c.shape) for c in consts] + slabs,
        out_specs=[tile] + slabs,
        out_shape=[jax.ShapeDtypeStruct(x.shape, x.dtype)]
        + [jax.ShapeDtypeStruct(w.shape, BF16) for w in ffn_weights],
        scratch_shapes=[
            pltpu.VMEM((tt, d), BF16),
            pltpu.VMEM((tt, d), BF16),
            pltpu.VMEM((tt, d), BF16),
            pltpu.VMEM((tt, d), BF16),
            pltpu.VMEM((tt, d), F32),
            pltpu.VMEM((tt, d), F32),
            pltpu.VMEM((2, tt, COL_CHUNK), F32),
            pltpu.VMEM((2, tt, COL_CHUNK), F32),
            pltpu.VMEM((2, tt + SUBLANES, COL_CHUNK), F32),
            pltpu.VMEM((SUBLANES, d), F32),
            pltpu.VMEM((1, d), F32),
            pltpu.VMEM((HG_HEADS, HG_DK, HG_DK), F32),
            pltpu.VMEM((HALVES, tt, LANES), F32),
            pltpu.VMEM((HALVES, tt, LANES), F32),
            pltpu.VMEM((HALVES, tt // SUBLANES, LANES), F32),
            pltpu.VMEM((N_D_CHUNKS, tt, COL_CHUNK), F32),
            pltpu.VMEM((N_D_CHUNKS, tt, COL_CHUNK), F32),
            pltpu.VMEM((N_D_CHUNKS, tt, COL_CHUNK), F32),
            pltpu.VMEM((N_D_CHUNKS, tt, COL_CHUNK), BF16),
            pltpu.VMEM((N_D_CHUNKS, tt, COL_CHUNK), F32),
        ],
        compiler_params=pltpu.CompilerParams(
            dimension_semantics=("arbitrary", "arbitrary"),
            vmem_limit_bytes=VMEM_LIMIT_BYTES),
        name="mixer",
    )(x, *consts, *ffn_weights)


def _ffn_call(x, *consts):
    bsz, t, d = x.shape
    tt = TIME_TILE
    tile = pl.BlockSpec((1, tt, d), lambda b, i: (b, i, 0))
    return pl.pallas_call(
        _ffn_kernel,
        grid=(bsz, t // tt),
        in_specs=[tile] + [_resident(c.shape) for c in consts],
        out_specs=tile,
        out_shape=jax.ShapeDtypeStruct(x.shape, x.dtype),
        scratch_shapes=[
            pltpu.VMEM((tt, d), BF16),
            pltpu.VMEM((tt, D_FF), F32),
            pltpu.VMEM((tt, D_FF), F32),
            pltpu.VMEM((4, tt + SUBLANES, COL_CHUNK), F32),
            pltpu.VMEM((SUBLANES, 2 * D_FF), F32),
        ],
        compiler_params=pltpu.CompilerParams(
            dimension_semantics=("arbitrary", "arbitrary"),
            vmem_limit_bytes=VMEM_LIMIT_BYTES),
        name="ffn",
    )(x, *consts)


def kernel(x, norm_pre_mix, w_in, conv_a_w, conv_a_b, lru_wa, lru_ba, lru_wx, lru_bx, lru_lambda,
           hg_lb_logits, hg_norm_g, w_branch_a, w_branch_b, w_out, norm_post_mix, norm_pre_ffn,
           w_up, conv_f_w, conv_f_b, w_down, norm_post_ffn):
    depth = w_in.shape[0]
    d = D_MODEL
    assert depth == 1 and hg_lb_logits.shape[0] == 2, "single-layer configuration only"
    for l in range(depth):
        x, w_up16, w_dn16 = _mixer_call(
            x, norm_pre_mix[l].reshape(1, d), w_in[l].astype(BF16),
            conv_a_w[l], conv_a_b[l].reshape(1, d),
            lru_wa[l].astype(BF16), lru_ba[l].reshape(1, d),
            lru_wx[l].astype(BF16), lru_bx[l].reshape(1, d),
            lru_lambda[l].reshape(1, d), hg_lb_logits, hg_norm_g[l].reshape(1, d),
            w_branch_a[l].astype(BF16), w_branch_b[l].astype(BF16), w_out[l].astype(BF16),
            norm_post_mix[l].reshape(1, d), ffn_weights=(w_up[l], w_down[l]))
        x = _ffn_call(
            x, norm_pre_ffn[l].reshape(1, d), w_up16,
            conv_f_w[l], conv_f_b[l].reshape(1, 2 * D_FF),
            w_dn16, norm_post_ffn[l].reshape(1, d))
    return x
```
